```python
import math
import jax, jax.numpy as jnp
from jax import lax
import numpy as np

D_MODEL = 1024
BATCH = 8
SEQ = 2048
DEPTH = 4
DEC_BATCH = 128
DEC_SEQ = 8
PAST_LEN = 8192
PAGE_SIZE = 128

A_HEADS = 4
A_DK = 128
A_DV = 128
A_WIDTH = A_HEADS * A_DV
A_CHUNK = 64
B_HEADS = 8
B_NOPE = 64
B_ROPE = 32
B_DV = 64
B_WIDTH = B_HEADS * B_DV
Q_LORA = 256
KV_LORA = 256
ROPE_THETA = 10000.0
Q_BLOCK = 128
D_FF = 4 * D_MODEL
EPS = 1e-6
MASK_VALUE = -1e30
F_FLOOR = 1e-30

SPLIT_SIZES = (A_HEADS * A_DK, A_HEADS * A_DK, A_WIDTH, A_WIDTH, Q_LORA, KV_LORA + B_ROPE, D_MODEL, D_MODEL)
N_IN = 2 * A_HEADS * A_DK + 2 * A_WIDTH + Q_LORA + KV_LORA + B_ROPE + 2 * D_MODEL

kernel_name = "hgrn2_mla_gated_hybrid_step"


def rmsnorm(x, g):
    xf = x.astype(jnp.float32)
    y = xf * lax.rsqrt(jnp.mean(xf * xf, axis=-1, keepdims=True) + EPS)
    return (y * g.astype(jnp.float32)).astype(x.dtype)


def rope(x, pos):
    half = x.shape[-1] // 2
    inv = ROPE_THETA ** (-jnp.arange(half, dtype=jnp.float32) / half)
    ang = pos.astype(jnp.float32)[:, None] * inv[None, :]
    cos = jnp.cos(ang)[:, None, :]
    sin = jnp.sin(ang)[:, None, :]
    xf = x.astype(jnp.float32)
    x1, x2 = xf[..., :half], xf[..., half:]
    return jnp.concatenate([x1 * cos - x2 * sin, x2 * cos + x1 * sin], axis=-1).astype(x.dtype)


def hgrn2_recurrence(q, k, v, log_f, s0):
    B, L, H, _ = q.shape
    C = min(A_CHUNK, L)
    n = -(-L // C)
    pad = n * C - L

    def prep(t):
        t = jnp.pad(t.astype(jnp.float32), ((0, 0), (0, pad), (0, 0), (0, 0)))
        return t.reshape(B, n, C, H, -1).transpose(1, 0, 3, 2, 4)

    qc, kc, vc, gc = prep(q), prep(k), prep(v), prep(log_f)
    causal = jnp.tril(jnp.ones((C, C), dtype=bool))[:, :, None]

    def step(S, inp):
        qb, kb, vb, gb = inp
        b = jnp.cumsum(gb, axis=2)
        diff = b[:, :, :, None, :] - b[:, :, None, :, :]
        decay = jnp.where(causal, jnp.exp(jnp.where(causal, diff, 0.0)), 0.0)
        scores = jnp.einsum('bhtk,bhsk,bhtsk->bhts', qb, kb, decay)
        o = jnp.einsum('bhts,bhsv->bhtv', scores, vb) + jnp.einsum('bhtk,bhkv->bhtv', qb * jnp.exp(b), S)
        b_end = b[:, :, -1:, :]
        S_new = jnp.exp(b_end[:, :, 0, :, None]) * S + jnp.einsum('bhsk,bhsv->bhkv', kb * jnp.exp(b_end - b), vb)
        return S_new, o

    S, o = lax.scan(step, s0.astype(jnp.float32), (qc, kc, vc, gc))
    o = o.transpose(1, 0, 3, 2, 4).reshape(B, n * C, H, -1)[:, :L]
    return o, S


def mla_attend(q_lat, q_pe, keys_c, keys_p, q_pos):
    B, Lq, H, R = q_lat.shape
    qb = min(Q_BLOCK, Lq)
    nb = Lq // qb
    k_pos = jnp.arange(keys_c.shape[1], dtype=jnp.int32)
    scale = (B_NOPE + B_ROPE) ** -0.5

    def block(args):
        ql, qp, pos = args
        s = (jnp.einsum('bqhr,bkr->bhqk', ql, keys_c, preferred_element_type=jnp.float32)
             + jnp.einsum('bqhd,bkd->bhqk', qp, keys_p, preferred_element_type=jnp.float32)) * scale
        s = jnp.where(k_pos[None, :] <= pos[:, None], s, MASK_VALUE)
        p = jax.nn.softmax(s, axis=-1).astype(keys_c.dtype)
        return jnp.einsum('bhqk,bkr->bqhr', p, keys_c)

    blocks = (q_lat.reshape(B, nb, qb, H, R).swapaxes(0, 1),
              q_pe.reshape(B, nb, qb, H, -1).swapaxes(0, 1),
              q_pos.reshape(nb, qb))
    o = lax.map(block, blocks)
    return o.swapaxes(0, 1).reshape(B, Lq, H, R)


def layer(x, pos0, s0, past_ckv, past_kpe, lb, norm1_g, w_in, a_norm_g, q_norm_g, w_uq,
          kv_norm_g, w_uk, w_uv, w_br_a, w_br_b, w_out, norm2_g, w_up, w_down):
    B, L, _ = x.shape
    f32 = jnp.float32
    pos = pos0 + jnp.arange(L, dtype=jnp.int32)
    h = rmsnorm(x, norm1_g)
    z = h @ w_in
    split_points = [int(v) for v in np.cumsum(SPLIT_SIZES)[:-1]]
    a_q, a_f, a_i, a_og, b_dq, b_dkv, g_a, g_b = jnp.split(z, split_points, axis=-1)

    heads = lambda t: t.reshape(B, L, A_HEADS, -1)
    fr = heads(a_f).astype(f32)
    lbh = lb.reshape(A_HEADS, A_DK).astype(f32)
    f_gate = lbh + (1.0 - lbh) * jax.nn.sigmoid(fr)
    log_f = jnp.log(jnp.maximum(f_gate, F_FLOOR))
    k_in = (1.0 - lbh) * jax.nn.sigmoid(-fr)
    q_a = jax.nn.silu(heads(a_q))
    o_a, s_new = hgrn2_recurrence(q_a, k_in, heads(a_i), log_f, s0)
    o_a = rmsnorm(o_a, a_norm_g).astype(x.dtype) * jax.nn.silu(heads(a_og))
    o_a = o_a.reshape(B, L, A_WIDTH)

    cq = rmsnorm(b_dq, q_norm_g)
    qf = (cq @ w_uq).reshape(B, L, B_HEADS, B_NOPE + B_ROPE)
    q_nope = qf[..., :B_NOPE]
    q_pe = rope(qf[..., B_NOPE:], pos)
    q_lat = jnp.einsum('blhn,hrn->blhr', q_nope, w_uk)
    ckv = rmsnorm(b_dkv[..., :KV_LORA], kv_norm_g)
    kpe = rope(b_dkv[..., None, KV_LORA:], pos)[:, :, 0]
    if past_ckv is None:
        keys_c, keys_p = ckv, kpe
    else:
        keys_c = jnp.concatenate([past_ckv, ckv], axis=1)
        keys_p = jnp.concatenate([past_kpe, kpe], axis=1)
    o_lat = mla_attend(q_lat, q_pe, keys_c, keys_p, pos)
    o_b = jnp.einsum('blhr,hrv->blhv', o_lat, w_uv).reshape(B, L, B_WIDTH)

    merged = jax.nn.sigmoid(g_a) * (o_a @ w_br_a) + jax.nn.sigmoid(g_b) * (o_b @ w_br_b)
    x = x + merged @ w_out

    h2 = rmsnorm(x, norm2_g)
    x = x + jnp.square(jax.nn.relu(h2 @ w_up)) @ w_down
    return x, s_new.astype(s0.dtype), ckv, kpe


def setup_inputs(seed: int = 0) -> dict:
    key = jax.random.key(seed)
    ks = jax.random.split(key, 24)
    f32 = jnp.float32
    n_pages = PAST_LEN // PAGE_SIZE
    n_used = DEC_BATCH * n_pages
    n_phys = n_used + n_used // 4
    nrm = lambda k, shape, s: jax.random.normal(k, shape, f32) * s
    page_table = jax.random.permutation(ks[5], n_phys)[:n_used].reshape(DEC_BATCH, n_pages).astype(jnp.int32)
    return {
        "x_prompt": nrm(ks[0], (BATCH, SEQ, D_MODEL), 1.0),
        "x_sample": nrm(ks[1], (DEC_BATCH, DEC_SEQ, D_MODEL), 1.0),
        "cache_ckv": nrm(ks[2], (DEPTH, n_phys, PAGE_SIZE, KV_LORA), 1.0),
        "cache_krope": nrm(ks[3], (DEPTH, n_phys, PAGE_SIZE, B_ROPE), 1.0),
        "state_hgrn": nrm(ks[4], (DEPTH, DEC_BATCH, A_HEADS, A_DK, A_DV), 0.3),
        "page_table": page_table,
        "norm1_g": 1.0 + nrm(ks[6], (DEPTH, D_MODEL), 0.02),
        "w_in": nrm(ks[7], (DEPTH, D_MODEL, N_IN), D_MODEL ** -0.5),
        "lower_bounds": nrm(ks[8], (DEPTH, A_HEADS * A_DK), 0.1),
        "a_norm_g": 1.0 + nrm(ks[9], (DEPTH, A_HEADS, A_DV), 0.02),
        "q_norm_g": 1.0 + nrm(ks[10], (DEPTH, Q_LORA), 0.02),
        "w_uq": nrm(ks[11], (DEPTH, Q_LORA, B_HEADS * (B_NOPE + B_ROPE)), Q_LORA ** -0.5),
        "kv_norm_g": 1.0 + nrm(ks[12], (DEPTH, KV_LORA), 0.02),
        "w_uk": nrm(ks[13], (DEPTH, B_HEADS, KV_LORA, B_NOPE), KV_LORA ** -0.5),
        "w_uv": nrm(ks[14], (DEPTH, B_HEADS, KV_LORA, B_DV), KV_LORA ** -0.5),
        "w_br_a": nrm(ks[15], (DEPTH, A_WIDTH, D_MODEL), A_WIDTH ** -0.5),
        "w_br_b": nrm(ks[16], (DEPTH, B_WIDTH, D_MODEL), B_WIDTH ** -0.5),
        "w_out": nrm(ks[17], (DEPTH, D_MODEL, D_MODEL), D_MODEL ** -0.5),
        "norm2_g": 1.0 + nrm(ks[18], (DEPTH, D_MODEL), 0.02),
        "w_up": nrm(ks[19], (DEPTH, D_MODEL, D_FF), D_MODEL ** -0.5),
        "w_down": nrm(ks[20], (DEPTH, D_FF, D_MODEL), D_FF ** -0.5),
        "final_norm_g": 1.0 + nrm(ks[21], (D_MODEL,), 0.02),
    }


def reference(x_prompt, x_sample, cache_ckv, cache_krope, state_hgrn, page_table,
              norm1_g, w_in, lower_bounds, a_norm_g, q_norm_g, w_uq, kv_norm_g, w_uk, w_uv,
              w_br_a, w_br_b, w_out, norm2_g, w_up, w_down, final_norm_g):
    p_lb = jax.nn.softmax(lower_bounds.astype(jnp.float32), axis=0)
    lbs = jnp.cumsum(p_lb, axis=0) - p_lb[0:1]
    n_seq = page_table.shape[0]

    xp, xs = x_prompt, x_sample
    ckv_p, kpe_p, hg_p, ckv_s, kpe_s, hg_s = [], [], [], [], [], []
    for l in range(DEPTH):
        w_l = (lbs[l], norm1_g[l], w_in[l], a_norm_g[l], q_norm_g[l], w_uq[l], kv_norm_g[l],
               w_uk[l], w_uv[l], w_br_a[l], w_br_b[l], w_out[l], norm2_g[l], w_up[l], w_down[l])
        s0_p = jnp.zeros((xp.shape[0], A_HEADS, A_DK, A_DV), xp.dtype)
        xp, s_p, c_p, k_p = layer(xp, 0, s0_p, None, None, *w_l)
        past_c = cache_ckv[l][page_table].reshape(n_seq, -1, KV_LORA)
        past_k = cache_krope[l][page_table].reshape(n_seq, -1, B_ROPE)
        xs, s_s, c_s, k_s = layer(xs, past_c.shape[1], state_hgrn[l], past_c, past_k, *w_l)
        ckv_p.append(c_p); kpe_p.append(k_p); hg_p.append(s_p)
        ckv_s.append(c_s); kpe_s.append(k_s); hg_s.append(s_s)

    y_prompt = rmsnorm(xp, final_norm_g)
    y_sample = rmsnorm(xs, final_norm_g)
    new_ckv_prompt = jnp.stack(ckv_p)
    new_krope_prompt = jnp.stack(kpe_p)
    new_hgrn_prompt = jnp.stack(hg_p)
    new_ckv_sample = jnp.stack(ckv_s)
    new_krope_sample = jnp.stack(kpe_s)
    new_hgrn_sample = jnp.stack(hg_s)
    return (y_prompt, y_sample, new_ckv_prompt, new_krope_prompt, new_hgrn_prompt,
            new_ckv_sample, new_krope_sample, new_hgrn_sample)
```

```python
import functools

import jax
import jax.numpy as jnp
import numpy as np
from jax import lax
from jax.experimental import pallas as pl
from jax.experimental.pallas import tpu as pltpu

D_MODEL = 1024
DEPTH = 4
PAGE_SIZE = 128
A_HEADS = 4
A_DK = 128
A_DV = 128
A_WIDTH = A_HEADS * A_DV
B_HEADS = 8
B_NOPE = 64
B_ROPE = 32
B_DV = 64
B_WIDTH = B_HEADS * B_DV
Q_LORA = 256
KV_LORA = 256
ROPE_THETA = 10000.0
D_FF = 4 * D_MODEL
EPS = 1e-6
MASK_VALUE = -1e30
F_FLOOR = 1e-30
SM_SCALE = (B_NOPE + B_ROPE) ** -0.5

LANES = 128
ROPE_T = B_HEADS * B_ROPE
VMEM_LIMIT = 56 * 1024 * 1024

BF16 = jnp.bfloat16
F32 = jnp.float32

_NT = (((1,), (1,)), ((), ()))
_TN = (((0,), (0,)), ((), ()))


def _dot(a, b):
    return jnp.dot(a, b, preferred_element_type=F32)


def _dot_nt(a, b):
    return lax.dot_general(a, b, _NT, preferred_element_type=F32)


def _dot_tn(a, b):
    return lax.dot_general(a, b, _TN, preferred_element_type=F32)


def _full_spec(shape):
    nd = len(shape)
    return pl.BlockSpec(shape, lambda *_: (0,) * nd)


def _params(sem):
    return pltpu.CompilerParams(dimension_semantics=sem, vmem_limit_bytes=VMEM_LIMIT)


def _rms(x, g):
    ms = jnp.mean(x * x, axis=-1, keepdims=True)
    return x * lax.rsqrt(ms + EPS) * g


def _lbs_kernel(lb_ref, out_ref):
    x = lb_ref[...]
    m = jnp.max(x, axis=0, keepdims=True)
    e = jnp.exp(x - m)
    p = e / jnp.sum(e, axis=0, keepdims=True)
    acc = jnp.zeros_like(p[0:1])
    out_ref[0:1, :] = acc
    for l in range(1, DEPTH):
        acc = acc + p[l:l + 1]
        out_ref[l:l + 1, :] = acc


def _lower_bounds(lower_bounds):
    return pl.pallas_call(
        _lbs_kernel,
        out_shape=jax.ShapeDtypeStruct(lower_bounds.shape, F32),
        name="hgrn_lower_bounds",
    )(lower_bounds.astype(F32))


def _proj_in_kernel(x_ref, g1_ref, wa_ref, wb_ref, wg_ref, qn_ref, kvn_ref, wqn_ref, wqp_ref,
                    wuk_ref, cos_ref, sin_ref,
                    qa_ref, fr_ref, v_ref, og_ref, ga_ref, gb_ref, qlat_ref, qpe_ref,
                    ckv_ref, kpe_ref, kcat_ref):
    h = _rms(x_ref[...], g1_ref[...]).astype(BF16)

    z = _dot(h, wa_ref[:, 0:A_WIDTH])
    qa_ref[...] = z * jax.nn.sigmoid(z)
    fr_ref[...] = _dot(h, wa_ref[:, A_WIDTH:2 * A_WIDTH])
    v_ref[...] = _dot(h, wa_ref[:, 2 * A_WIDTH:3 * A_WIDTH])
    z = _dot(h, wa_ref[:, 3 * A_WIDTH:4 * A_WIDTH])
    og_ref[...] = z * jax.nn.sigmoid(z)

    half = D_MODEL // 2
    for c in range(2):
        ga_ref[:, c * half:(c + 1) * half] = jax.nn.sigmoid(
            _dot(h, wg_ref[:, c * half:(c + 1) * half])).astype(ga_ref.dtype)
        gb_ref[:, c * half:(c + 1) * half] = jax.nn.sigmoid(
            _dot(h, wg_ref[:, D_MODEL + c * half:D_MODEL + (c + 1) * half])).astype(gb_ref.dtype)

    zb = _dot(h, wb_ref[...])
    cos = cos_ref[...]
    sin = sin_ref[...]
    cq = _rms(zb[:, 0:Q_LORA], qn_ref[...]).astype(BF16)
    ckv = _rms(zb[:, Q_LORA:Q_LORA + KV_LORA], kvn_ref[...])
    off = Q_LORA + KV_LORA
    kpe = zb[:, off:off + ROPE_T] * cos + zb[:, off + ROPE_T:off + 2 * ROPE_T] * sin
    ckv_ref[...] = ckv
    kpe_ref[...] = kpe[:, 0:B_ROPE]
    kcat_ref[:, 0:KV_LORA] = ckv.astype(kcat_ref.dtype)
    kcat_ref[:, KV_LORA:KV_LORA + ROPE_T] = kpe.astype(kcat_ref.dtype)

    qp2 = _dot(cq, wqp_ref[...])
    qpe = qp2[:, 0:ROPE_T] * cos + qp2[:, ROPE_T:2 * ROPE_T] * sin
    qpe_ref[...] = (qpe * SM_SCALE).astype(qpe_ref.dtype)
    qn = _dot(cq, wqn_ref[...]).astype(BF16)
    for hh in range(B_HEADS):
        ql = _dot(qn[:, hh * LANES:(hh + 1) * LANES], wuk_ref[hh])
        qlat_ref[hh] = (ql * SM_SCALE).astype(qlat_ref.dtype)


def _proj_in(x, w, cos_tab, sin_tab, tm, act_dtype):
    T = x.shape[0]
    n_tab = cos_tab.shape[0] // tm
    grid = (T // tm,)
    row = lambda i: (i, 0)
    tab = lambda i: (i % n_tab, 0)
    in_specs = [
        pl.BlockSpec((tm, D_MODEL), row),
        _full_spec((1, D_MODEL)),
        _full_spec(w["wa"].shape), _full_spec(w["wb"].shape), _full_spec(w["wg"].shape),
        _full_spec((1, Q_LORA)), _full_spec((1, KV_LORA)),
        _full_spec(w["wqn"].shape), _full_spec(w["wqp"].shape), _full_spec(w["wuk"].shape),
        pl.BlockSpec((tm, ROPE_T), tab), pl.BlockSpec((tm, ROPE_T), tab),
    ]
    out_shape = [
        jax.ShapeDtypeStruct((T, A_WIDTH), F32),
        jax.ShapeDtypeStruct((T, A_WIDTH), F32),
        jax.ShapeDtypeStruct((T, A_WIDTH), F32),
        jax.ShapeDtypeStruct((T, A_WIDTH), F32),
        jax.ShapeDtypeStruct((T, D_MODEL), BF16),
        jax.ShapeDtypeStruct((T, D_MODEL), BF16),
        jax.ShapeDtypeStruct((B_HEADS, T, KV_LORA), act_dtype),
        jax.ShapeDtypeStruct((T, ROPE_T), act_dtype),
        jax.ShapeDtypeStruct((T, KV_LORA), F32),
        jax.ShapeDtypeStruct((T, B_ROPE), F32),
        jax.ShapeDtypeStruct((T, KV_LORA + ROPE_T), act_dtype),
    ]
    out_specs = [
        pl.BlockSpec((tm, A_WIDTH), row), pl.BlockSpec((tm, A_WIDTH), row),
        pl.BlockSpec((tm, A_WIDTH), row), pl.BlockSpec((tm, A_WIDTH), row),
        pl.BlockSpec((tm, D_MODEL), row), pl.BlockSpec((tm, D_MODEL), row),
        pl.BlockSpec((B_HEADS, tm, KV_LORA), lambda i: (0, i, 0)),
        pl.BlockSpec((tm, ROPE_T), row),
        pl.BlockSpec((tm, KV_LORA), row),
        pl.BlockSpec((tm, B_ROPE), row),
        pl.BlockSpec((tm, KV_LORA + ROPE_T), row),
    ]
    return pl.pallas_call(
        _proj_in_kernel,
        grid=grid, in_specs=in_specs, out_specs=out_specs, out_shape=out_shape,
        compiler_params=_params(("parallel",)),
        name="proj_in",
    )(x, w["g1"], w["wa"], w["wb"], w["wg"], w["qn"], w["kvn"], w["wqn"], w["wqp"], w["wuk"],
      cos_tab, sin_tab)


def _cumsum_rows(g):
    n = g.shape[0]
    row = lax.broadcasted_iota(jnp.int32, g.shape, 0)
    b = g
    shift = 1
    while shift < n:
        b = b + jnp.where(row >= shift, pltpu.roll(b, shift, axis=0), 0.0)
        shift *= 2
    return b


def _hgrn_chunk(q, fr, v, lb, st, chunk, sub, md):
    n_sub = chunk // sub
    one_m_lb = 1.0 - lb
    f_gate = lb + one_m_lb * jax.nn.sigmoid(fr)
    g = jnp.log(jnp.maximum(f_gate, F_FLOOR))
    k = one_m_lb * jax.nn.sigmoid(-fr)
    b = _cumsum_rows(g)
    vb = v.astype(md)

    o = _dot_nt((q * jnp.exp(b)).astype(md), st.astype(md))
    b_end = b[chunk - 1:chunk, :]
    kd = (k * jnp.exp(b_end - b)).astype(md)
    st_new = st * jnp.exp(b_end) + _dot(v.T.astype(md), kd)

    if n_sub > 1:
        parts = [jnp.zeros((sub, A_DV), F32)]
        for i in range(1, n_sub):
            lo = i * sub
            r_i = b[lo - 1:lo, :]
            q_i = (q[lo:lo + sub] * jnp.exp(b[lo:lo + sub] - r_i)).astype(md)
            k_i = (k[0:lo] * jnp.exp(r_i - b[0:lo])).astype(md)
            a_i = _dot_nt(q_i, k_i).astype(md)
            parts.append(_dot(a_i, vb[0:lo]))
        o = o + jnp.concatenate(parts, axis=0)

    q3 = q.reshape(n_sub, sub, A_DK)
    k3 = k.reshape(n_sub, sub, A_DK)
    b3 = b.reshape(n_sub, sub, A_DK)
    v3 = vb.astype(F32).reshape(n_sub, sub, A_DV)
    tpos = lax.broadcasted_iota(jnp.int32, (n_sub, sub, A_DK), 1)
    od = jnp.zeros((n_sub, sub, A_DV), F32)
    for s in range(sub):
        mask = tpos >= s
        d = jnp.where(mask, b3 - b3[:, s:s + 1, :], 0.0)
        wgt = jnp.where(mask, q3 * k3[:, s:s + 1, :] * jnp.exp(d), 0.0)
        od = od + jnp.sum(wgt, axis=-1, keepdims=True) * v3[:, s:s + 1, :]
    return o + od.reshape(chunk, A_DV), st_new


def _hgrn_kernel(*refs, chunk, sub, has_s0):
    if has_s0:
        (qa_ref, fr_ref, v_ref, og_ref, lb_ref, ang_ref, s0_ref, oa_ref, sout_ref, st_ref) = refs
    else:
        (qa_ref, fr_ref, v_ref, og_ref, lb_ref, ang_ref, oa_ref, sout_ref, st_ref) = refs
        s0_ref = None
    j = pl.program_id(1)
    nb, lblk = qa_ref.shape[0], qa_ref.shape[1]
    n_chunks = lblk // chunk

    @pl.when(j == 0)
    def _():
        if has_s0:
            for n in range(nb):
                for hh in range(A_HEADS):
                    st_ref[n, hh] = s0_ref[n, hh].T
        else:
            st_ref[...] = jnp.zeros_like(st_ref)

    md = BF16 if chunk >= 16 else F32

    def seq_body(n, carry):
        def chunk_body(c, carry2):
            rows = pl.ds(pl.multiple_of(c * chunk, chunk), chunk)
            for hh in range(A_HEADS):
                cols = slice(hh * LANES, (hh + 1) * LANES)
                o, st_new = _hgrn_chunk(qa_ref[n, rows, cols], fr_ref[n, rows, cols], v_ref[n, rows, cols],
                                        lb_ref[:, cols], st_ref[n, hh], chunk, sub, md)
                st_ref[n, hh] = st_new
                on = _rms(o, ang_ref[:, cols])
                oa_ref[n, rows, cols] = (on * og_ref[n, rows, cols]).astype(oa_ref.dtype)
            return carry2
        return lax.fori_loop(0, n_chunks, chunk_body, carry)
    lax.fori_loop(0, nb, seq_body, 0)

    @pl.when(j == pl.num_programs(1) - 1)
    def _():
        for n in range(nb):
            for hh in range(A_HEADS):
                sout_ref[n, hh] = st_ref[n, hh].T


def _hgrn(qa, fr, v, og, lb, ang, s0, layer, nb, lblk, chunk, sub):
    B, L, _ = qa.shape
    has_s0 = s0 is not None
    grid = (B // nb, L // lblk)
    act = pl.BlockSpec((nb, lblk, A_WIDTH), lambda i, j: (i, j, 0))
    st_spec = pl.BlockSpec((nb, A_HEADS, A_DK, A_DV), lambda i, j: (i, 0, 0, 0))
    in_specs = [act, act, act, act, _full_spec((1, A_WIDTH)), _full_spec((1, A_WIDTH))]
    args = [qa, fr, v, og, lb, ang]
    if has_s0:
        in_specs.append(pl.BlockSpec((None, nb, A_HEADS, A_DK, A_DV), lambda i, j: (layer, i, 0, 0, 0)))
        args.append(s0)
    return pl.pallas_call(
        functools.partial(_hgrn_kernel, chunk=chunk, sub=sub, has_s0=has_s0),
        grid=grid, in_specs=in_specs,
        out_specs=[act, st_spec],
        out_shape=[jax.ShapeDtypeStruct((B, L, A_WIDTH), BF16),
                   jax.ShapeDtypeStruct((B, A_HEADS, A_DK, A_DV), F32)],
        scratch_shapes=[pltpu.VMEM((nb, A_HEADS, A_DV, A_DK), F32)],
        compiler_params=_params(("parallel", "arbitrary")),
        name="hgrn_sample" if has_s0 else "hgrn_prompt",
    )(*args)


def _latent_to_heads(o, wuv_ref, out_ref, rows):
    md = BF16 if rows >= 16 else F32
    ob = o.astype(md)
    for p in range(B_HEADS // 2):
        pair = jnp.concatenate([ob[(2 * p) * rows:(2 * p + 1) * rows],
                                ob[(2 * p + 1) * rows:(2 * p + 2) * rows]], axis=-1)
        out_ref[:, p * LANES:(p + 1) * LANES] = _dot(pair, wuv_ref[p].astype(md)).astype(out_ref.dtype)


def _attn_prompt_kernel(qlat_ref, qpe_ref, kcat_ref, wuv_ref, out_ref, q_s, m_s, l_s, acc_s, *, tq):
    qi = pl.program_id(1)
    rows = B_HEADS * tq
    qpe = qpe_ref[...].astype(F32)
    lane = lax.broadcasted_iota(jnp.int32, (tq, ROPE_T), 1)
    for hh in range(B_HEADS):
        own = (lane >= hh * B_ROPE) & (lane < (hh + 1) * B_ROPE)
        q_s[hh * tq:(hh + 1) * tq, 0:KV_LORA] = qlat_ref[hh]
        q_s[hh * tq:(hh + 1) * tq, KV_LORA:KV_LORA + ROPE_T] = jnp.where(own, qpe, 0.0).astype(BF16)
    m_s[...] = jnp.full_like(m_s, -jnp.inf)
    l_s[...] = jnp.zeros_like(l_s)
    acc_s[...] = jnp.zeros_like(acc_s)

    def tile(kt, diagonal):
        kk = kcat_ref[pl.ds(pl.multiple_of(kt * tq, tq), tq), :]
        s = _dot_nt(q_s[...], kk)
        if diagonal:
            s3 = s.reshape(B_HEADS, tq, tq)
            qpos = lax.broadcasted_iota(jnp.int32, (B_HEADS, tq, tq), 1)
            kpos = lax.broadcasted_iota(jnp.int32, (B_HEADS, tq, tq), 2)
            s = jnp.where(kpos <= qpos, s3, MASK_VALUE).reshape(rows, tq)
        m_prev = m_s[...]
        m_new = jnp.maximum(m_prev, jnp.max(s, axis=-1, keepdims=True))
        alpha = jnp.exp(m_prev - m_new)
        p = jnp.exp(s - m_new)
        l_s[...] = alpha * l_s[...] + jnp.sum(p, axis=-1, keepdims=True)
        acc_s[...] = alpha * acc_s[...] + _dot(p.astype(BF16), kk[:, 0:KV_LORA])
        m_s[...] = m_new

    def body(kt, carry):
        tile(kt, False)
        return carry
    lax.fori_loop(0, qi, body, 0)
    tile(qi, True)
    _latent_to_heads(acc_s[...] / l_s[...], wuv_ref, out_ref, tq)


def _attn_prompt(qlat, qpe, kcat, wuv, B, L, tq):
    T = B * L
    nq = L // tq
    rows = B_HEADS * tq
    kw = KV_LORA + ROPE_T
    return pl.pallas_call(
        functools.partial(_attn_prompt_kernel, tq=tq),
        grid=(B, nq),
        in_specs=[
            pl.BlockSpec((B_HEADS, tq, KV_LORA), lambda b, i: (0, b * nq + i, 0)),
            pl.BlockSpec((tq, ROPE_T), lambda b, i: (b * nq + i, 0)),
            pl.BlockSpec((None, L, kw), lambda b, i: (b, 0, 0)),
            _full_spec(wuv.shape),
        ],
        out_specs=pl.BlockSpec((tq, B_WIDTH), lambda b, i: (b * nq + i, 0)),
        out_shape=jax.ShapeDtypeStruct((T, B_WIDTH), BF16),
        scratch_shapes=[pltpu.VMEM((rows, kw), BF16), pltpu.VMEM((rows, 1), F32),
                        pltpu.VMEM((rows, 1), F32), pltpu.VMEM((rows, KV_LORA), F32)],
        compiler_params=_params(("parallel", "arbitrary")),
        name="attn_prompt",
    )(qlat, qpe, kcat.reshape(B, L, kw), wuv)


def _attn_sample_kernel(pt_ref, qlat_ref, qpe_ref, knew_ref, wuv_ref, *refs, n_pages, dec):
    ckv_refs = refs[0:n_pages]
    kr_refs = refs[n_pages:2 * n_pages]
    out_ref = refs[2 * n_pages]
    q_s, qp_s, k_s, kp_s, m_s, l_s, acc_s = refs[2 * n_pages + 1:]
    g = pl.program_id(1)
    rows = B_HEADS * dec

    @pl.when(g == 0)
    def _():
        q_s[...] = qlat_ref[...].reshape(rows, KV_LORA).astype(BF16)
        qpe = qpe_ref[...]
        qp_s[...] = jnp.concatenate(
            [qpe[:, hh * B_ROPE:(hh + 1) * B_ROPE] for hh in range(B_HEADS)], axis=0).astype(BF16)
        m_s[...] = jnp.full_like(m_s, -jnp.inf)
        l_s[...] = jnp.zeros_like(l_s)
        acc_s[...] = jnp.zeros_like(acc_s)

    def update(s, vals):
        m_prev = m_s[...]
        m_new = jnp.maximum(m_prev, jnp.max(s, axis=-1, keepdims=True))
        alpha = jnp.exp(m_prev - m_new)
        p = jnp.exp(s - m_new)
        l_s[...] = alpha * l_s[...] + jnp.sum(p, axis=-1, keepdims=True)
        acc_s[...] = alpha * acc_s[...] + _dot(p.astype(vals.dtype), vals)
        m_s[...] = m_new

    for i in range(n_pages):
        k_s[i * PAGE_SIZE:(i + 1) * PAGE_SIZE, :] = ckv_refs[i][...].astype(BF16)
        kp_s[i * PAGE_SIZE:(i + 1) * PAGE_SIZE, :] = kr_refs[i][...].astype(BF16)
    kc = k_s[...]
    update(_dot_nt(q_s[...], kc) + _dot_nt(qp_s[...], kp_s[...]), kc)

    @pl.when(g == pl.num_programs(1) - 1)
    def _():
        knew = knew_ref[...]
        kc_new = knew[:, 0:KV_LORA]
        kp_new = knew[:, KV_LORA:KV_LORA + B_ROPE]
        s = _dot_nt(q_s[...].astype(F32), kc_new) + _dot_nt(qp_s[...].astype(F32), kp_new)
        qpos = lax.broadcasted_iota(jnp.int32, (rows, dec), 0) % dec
        kpos = lax.broadcasted_iota(jnp.int32, (rows, dec), 1)
        update(jnp.where(kpos <= qpos, s, MASK_VALUE), kc_new)
        _latent_to_heads(acc_s[...] / l_s[...], wuv_ref, out_ref, dec)


def _attn_sample(page_table, qlat, qpe, kcat, cache_ckv, cache_krope, wuv, layer, n_pages):
    n_seq, n_logical = page_table.shape
    dec = qlat.shape[1] // n_seq
    rows = B_HEADS * dec
    kw = KV_LORA + ROPE_T
    groups = n_logical // n_pages

    def page_spec(width, i):
        return pl.BlockSpec((None, None, PAGE_SIZE, width),
                            lambda n, g, pt: (layer, pt[n, g * n_pages + i], 0, 0))
    in_specs = [
        pl.BlockSpec((B_HEADS, None, dec, KV_LORA), lambda n, g, pt: (0, n, 0, 0)),
        pl.BlockSpec((None, dec, ROPE_T), lambda n, g, pt: (n, 0, 0)),
        pl.BlockSpec((None, dec, kw), lambda n, g, pt: (n, 0, 0)),
        pl.BlockSpec(wuv.shape, lambda n, g, pt: (0, 0, 0)),
    ]
    in_specs += [page_spec(KV_LORA, i) for i in range(n_pages)]
    in_specs += [page_spec(B_ROPE, i) for i in range(n_pages)]
    grid_spec = pltpu.PrefetchScalarGridSpec(
        num_scalar_prefetch=1, grid=(n_seq, groups), in_specs=in_specs,
        out_specs=pl.BlockSpec((None, dec, B_WIDTH), lambda n, g, pt: (n, 0, 0)),
        scratch_shapes=[pltpu.VMEM((rows, KV_LORA), BF16), pltpu.VMEM((rows, B_ROPE), BF16),
                        pltpu.VMEM((n_pages * PAGE_SIZE, KV_LORA), BF16),
                        pltpu.VMEM((n_pages * PAGE_SIZE, B_ROPE), BF16),
                        pltpu.VMEM((rows, 1), F32), pltpu.VMEM((rows, 1), F32),
                        pltpu.VMEM((rows, KV_LORA), F32)])
    out = pl.pallas_call(
        functools.partial(_attn_sample_kernel, n_pages=n_pages, dec=dec),
        grid_spec=grid_spec,
        out_shape=jax.ShapeDtypeStruct((n_seq, dec, B_WIDTH), F32),
        compiler_params=_params(("parallel", "arbitrary")),
        name="attn_sample",
    )(page_table, qlat.reshape(B_HEADS, n_seq, dec, KV_LORA), qpe.reshape(n_seq, dec, ROPE_T),
      kcat.reshape(n_seq, dec, kw), wuv,
      *([cache_ckv] * n_pages), *([cache_krope] * n_pages))
    return out.reshape(n_seq * dec, B_WIDTH)


def _post_kernel(x_ref, oa_ref, ob_ref, ga_ref, gb_ref, wbra_ref, wbrb_ref, wout_ref, g2_ref,
                 wup_ref, wdn_ref, gf_ref, out_ref, *, final):
    ma = _dot(oa_ref[...].astype(BF16), wbra_ref[...])
    mb = _dot(ob_ref[...].astype(BF16), wbrb_ref[...])
    merged = ga_ref[...].astype(F32) * ma + gb_ref[...].astype(F32) * mb
    x1 = x_ref[...] + _dot(merged.astype(BF16), wout_ref[...])
    h2 = _rms(x1, g2_ref[...]).astype(BF16)
    acc = x1
    ff_chunk = D_MODEL
    for c in range(D_FF // ff_chunk):
        u = jnp.maximum(_dot(h2, wup_ref[:, c * ff_chunk:(c + 1) * ff_chunk]), 0.0)
        acc = acc + _dot((u * u).astype(BF16), wdn_ref[c * ff_chunk:(c + 1) * ff_chunk, :])
    if final:
        acc = _rms(acc, gf_ref[...])
    out_ref[...] = acc


def _post(x, oa, ob, ga, gb, w, gf, tm, final):
    T = x.shape[0]
    row = lambda i: (i, 0)
    in_specs = [
        pl.BlockSpec((tm, D_MODEL), row), pl.BlockSpec((tm, A_WIDTH), row), pl.BlockSpec((tm, B_WIDTH), row),
        pl.BlockSpec((tm, D_MODEL), row), pl.BlockSpec((tm, D_MODEL), row),
        _full_spec(w["wbra"].shape), _full_spec(w["wbrb"].shape), _full_spec(w["wout"].shape),
        _full_spec((1, D_MODEL)), _full_spec(w["wup"].shape), _full_spec(w["wdn"].shape),
        _full_spec((1, D_MODEL)),
    ]
    return pl.pallas_call(
        functools.partial(_post_kernel, final=final),
        grid=(T // tm,), in_specs=in_specs,
        out_specs=pl.BlockSpec((tm, D_MODEL), row),
        out_shape=jax.ShapeDtypeStruct((T, D_MODEL), F32),
        compiler_params=_params(("parallel",)),
        name="post_final" if final else "post",
    )(x, oa, ob, ga, gb, w["wbra"], w["wbrb"], w["wout"], w["g2"], w["wup"], w["wdn"], gf)


def _rot_half(w):
    half = w.shape[-1] // 2
    return jnp.concatenate([-w[..., half:], w[..., :half]], axis=-1)


def _layer_weights(l, lbs, norm1_g, w_in, a_norm_g, q_norm_g, w_uq, kv_norm_g, w_uk, w_uv,
                   w_br_a, w_br_b, w_out, norm2_g, w_up, w_down):
    wi = w_in[l]
    c_b = 4 * A_WIDTH
    c_rope = c_b + Q_LORA + KV_LORA
    c_g = c_rope + B_ROPE
    rope = wi[:, c_rope:c_g]
    wb = jnp.concatenate([wi[:, c_b:c_rope], jnp.tile(rope, (1, B_HEADS)),
                          jnp.tile(_rot_half(rope), (1, B_HEADS))], axis=-1)
    uq = w_uq[l].reshape(Q_LORA, B_HEADS, B_NOPE + B_ROPE)
    wqn = jnp.pad(uq[:, :, :B_NOPE], ((0, 0), (0, 0), (0, LANES - B_NOPE))).reshape(Q_LORA, B_HEADS * LANES)
    pe = uq[:, :, B_NOPE:]
    wqp = jnp.concatenate([pe.reshape(Q_LORA, ROPE_T), _rot_half(pe).reshape(Q_LORA, ROPE_T)], axis=-1)
    wuk = jnp.pad(jnp.swapaxes(w_uk[l], 1, 2), ((0, 0), (0, LANES - B_NOPE), (0, 0)))
    uv = w_uv[l]
    zeros = jnp.zeros_like(uv[0])
    wuv = jnp.stack([jnp.concatenate([jnp.concatenate([uv[2 * p], zeros], axis=-1),
                                      jnp.concatenate([zeros, uv[2 * p + 1]], axis=-1)], axis=0)
                     for p in range(B_HEADS // 2)])
    bf = lambda a: a.astype(BF16)
    return dict(
        g1=norm1_g[l].reshape(1, D_MODEL), wa=bf(wi[:, :c_b]), wb=bf(wb), wg=bf(wi[:, c_g:]),
        qn=q_norm_g[l].reshape(1, Q_LORA), kvn=kv_norm_g[l].reshape(1, KV_LORA),
        wqn=bf(wqn), wqp=bf(wqp), wuk=bf(wuk), wuv=bf(wuv),
        lb=lbs[l].reshape(1, A_WIDTH), ang=a_norm_g[l].reshape(1, A_WIDTH),
        wbra=bf(w_br_a[l]), wbrb=bf(w_br_b[l]), wout=bf(w_out[l]),
        g2=norm2_g[l].reshape(1, D_MODEL), wup=bf(w_up[l]), wdn=bf(w_down[l]),
    )


def _rope_tables(pos):
    half = B_ROPE // 2
    inv = ROPE_THETA ** (-jnp.arange(half, dtype=F32) / half)
    ang = pos.astype(F32)[:, None] * inv[None, :]
    cos = jnp.tile(jnp.cos(ang), (1, 2 * B_HEADS))
    sin = jnp.tile(jnp.sin(ang), (1, 2 * B_HEADS))
    return cos, sin


def kernel(x_prompt, x_sample, cache_ckv, cache_krope, state_hgrn, page_table, norm1_g, w_in, lower_bounds,
           a_norm_g, q_norm_g, w_uq, kv_norm_g, w_uk, w_uv, w_br_a, w_br_b, w_out, norm2_g, w_up, w_down,
           final_norm_g):
    B, L, _ = x_prompt.shape
    n_seq, dec, _ = x_sample.shape
    past_len = page_table.shape[1] * PAGE_SIZE
    tm_p, tm_s, tq = 256, 256, 256
    pages_per_step = 8

    lbs = _lower_bounds(lower_bounds)
    cos_p, sin_p = _rope_tables(jnp.arange(L, dtype=jnp.int32))
    pos_s = past_len + (jnp.arange(tm_s, dtype=jnp.int32) % dec)
    cos_s, sin_s = _rope_tables(pos_s)
    gf = final_norm_g.reshape(1, D_MODEL)

    xp = x_prompt.reshape(B * L, D_MODEL)
    xs = x_sample.reshape(n_seq * dec, D_MODEL)
    outs = [[] for _ in range(6)]
    for l in range(DEPTH):
        w = _layer_weights(l, lbs, norm1_g, w_in, a_norm_g, q_norm_g, w_uq, kv_norm_g, w_uk, w_uv,
                           w_br_a, w_br_b, w_out, norm2_g, w_up, w_down)
        final = l == DEPTH - 1

        qa, fr, v, og, ga, gb, qlat, qpe, ckv, kpe, kcat = _proj_in(xp, w, cos_p, sin_p, tm_p, BF16)
        to_seq = lambda a: a.reshape(B, L, A_WIDTH)
        oa, s_p = _hgrn(to_seq(qa), to_seq(fr), to_seq(v), to_seq(og), w["lb"], w["ang"], None, l,
                        nb=1, lblk=256, chunk=64, sub=16)
        ob = _attn_prompt(qlat, qpe, kcat, w["wuv"], B, L, tq)
        xp = _post(xp, oa.reshape(B * L, A_WIDTH), ob, ga, gb, w, gf, tm_p, final)
        outs[0].append(ckv.reshape(B, L, KV_LORA))
        outs[1].append(kpe.reshape(B, L, B_ROPE))
        outs[2].append(s_p)

        qa, fr, v, og, ga, gb, qlat, qpe, ckv, kpe, kcat = _proj_in(xs, w, cos_s, sin_s, tm_s, F32)
        to_seq = lambda a: a.reshape(n_seq, dec, A_WIDTH)
        oa, s_s = _hgrn(to_seq(qa), to_seq(fr), to_seq(v), to_seq(og), w["lb"], w["ang"], state_hgrn, l,
                        nb=8, lblk=dec, chunk=dec, sub=dec)
        ob = _attn_sample(page_table, qlat, qpe, kcat, cache_ckv, cache_krope, w["wuv"], l, pages_per_step)
        xs = _post(xs, oa.reshape(n_seq * dec, A_WIDTH), ob, ga, gb, w, gf, tm_s, final)
        outs[3].append(ckv.reshape(n_seq, dec, KV_LORA))
        outs[4].append(kpe.reshape(n_seq, dec, B_ROPE))
        outs[5].append(s_s)

    return (xp.reshape(B, L, D_MODEL), xs.reshape(n_seq, dec, D_MODEL),
            jnp.stack(outs[0]), jnp.stack(outs[1]), jnp.stack(outs[2]),
            jnp.stack(outs[3]), jnp.stack(outs[4]), jnp.stack(outs[5]))
```

```python
import functools

import jax
import jax.numpy as jnp
from jax import lax
from jax.experimental import pallas as pl
from jax.experimental.pallas import tpu as pltpu

D_MODEL = 1024
DEPTH = 4
PAGE_SIZE = 128
A_HEADS = 4
A_DK = 128
A_DV = 128
A_WIDTH = A_HEADS * A_DV
B_HEADS = 8
B_NOPE = 64
B_ROPE = 32
B_DV = 64
B_WIDTH = B_HEADS * B_DV
Q_LORA = 256
KV_LORA = 256
ROPE_THETA = 10000.0
D_FF = 4 * D_MODEL
EPS = 1e-6
MASK_VALUE = -1e30
F_FLOOR = 1e-30
SM_SCALE = (B_NOPE + B_ROPE) ** -0.5
LOG2E = 1.4426950408889634

LANES = 128
PACKED_ROWS = 16
ROPE_T = B_HEADS * B_ROPE
KCAT = KV_LORA + ROPE_T
VMEM_LIMIT = 56 * 1024 * 1024

BF16 = jnp.bfloat16
F32 = jnp.float32

_NT = (((1,), (1,)), ((), ()))


def _dot(a, b):
    return jnp.dot(a, b, preferred_element_type=F32)


def _dot_nt(a, b):
    return lax.dot_general(a, b, _NT, preferred_element_type=F32)


def _full_spec(shape):
    nd = len(shape)
    return pl.BlockSpec(shape, lambda *_: (0,) * nd)


def _params(sem):
    return pltpu.CompilerParams(dimension_semantics=sem, vmem_limit_bytes=VMEM_LIMIT)


def _rms(x, g):
    ms = jnp.mean(x * x, axis=-1, keepdims=True)
    return x * lax.rsqrt(ms + EPS) * g


def _operand_dtype(rows):
    return BF16 if rows >= PACKED_ROWS else F32


def _lbs_kernel(lb_ref, out_ref):
    x = lb_ref[...]
    m = jnp.max(x, axis=0, keepdims=True)
    e = jnp.exp(x - m)
    p = e / jnp.sum(e, axis=0, keepdims=True)
    acc = jnp.zeros_like(p[0:1])
    out_ref[0:1, :] = acc
    for l in range(1, DEPTH):
        acc = acc + p[l:l + 1]
        out_ref[l:l + 1, :] = acc


def _lower_bounds(lower_bounds):
    return pl.pallas_call(
        _lbs_kernel,
        out_shape=jax.ShapeDtypeStruct(lower_bounds.shape, F32),
        name="hgrn_lower_bounds",
    )(lower_bounds.astype(F32))


def _proj_in_kernel(x_ref, g1_ref, wa_ref, wb_ref, wg_ref, qn_ref, kvn_ref, wqn_ref, wqp_ref,
                    wuk_ref, cos_ref, sin_ref,
                    qa_ref, fr_ref, v_ref, og_ref, ga_ref, gb_ref, ckv_ref, kpe_ref, *mla_refs, prompt):
    tm = x_ref.shape[0]
    h = _rms(x_ref[...], g1_ref[...]).astype(BF16)

    z = _dot(h, wa_ref[:, 0:A_WIDTH])
    qa_ref[...] = z * jax.nn.sigmoid(z)
    fr_ref[...] = _dot(h, wa_ref[:, A_WIDTH:2 * A_WIDTH])
    v_ref[...] = _dot(h, wa_ref[:, 2 * A_WIDTH:3 * A_WIDTH])
    z = _dot(h, wa_ref[:, 3 * A_WIDTH:4 * A_WIDTH])
    og_ref[...] = z * jax.nn.sigmoid(z)

    half = D_MODEL // 2
    for c in range(2):
        ga_ref[:, c * half:(c + 1) * half] = jax.nn.sigmoid(
            _dot(h, wg_ref[:, c * half:(c + 1) * half])).astype(ga_ref.dtype)
        gb_ref[:, c * half:(c + 1) * half] = jax.nn.sigmoid(
            _dot(h, wg_ref[:, D_MODEL + c * half:D_MODEL + (c + 1) * half])).astype(gb_ref.dtype)

    zb = _dot(h, wb_ref[...])
    cos = cos_ref[...]
    sin = sin_ref[...]
    cq = _rms(zb[:, 0:Q_LORA], qn_ref[...]).astype(BF16)
    ckv = _rms(zb[:, Q_LORA:Q_LORA + KV_LORA], kvn_ref[...])
    off = Q_LORA + KV_LORA
    kpe = zb[:, off:off + ROPE_T] * cos + zb[:, off + ROPE_T:off + 2 * ROPE_T] * sin
    ckv_ref[...] = ckv
    kpe_ref[...] = kpe[:, 0:B_ROPE]

    qp2 = _dot(cq, wqp_ref[...])
    qpe = (qp2[:, 0:ROPE_T] * cos + qp2[:, ROPE_T:2 * ROPE_T] * sin) * SM_SCALE
    qn = _dot(cq, wqn_ref[...]).astype(BF16)

    if prompt:
        qt_ref, kcat_ref, ckvt_ref = mla_refs
        kcat_ref[:, 0:KV_LORA] = ckv.astype(BF16)
        kcat_ref[:, KV_LORA:KCAT] = kpe.astype(BF16)
        ckvt_ref[...] = ckv.T.astype(BF16)
        qpe_t = qpe.T
        row_head = lax.broadcasted_iota(jnp.int32, (ROPE_T, tm), 0) // B_ROPE
        for hh in range(B_HEADS):
            ql = _dot(qn[:, hh * LANES:(hh + 1) * LANES], wuk_ref[hh]) * SM_SCALE
            qt_ref[0:KV_LORA, hh * tm:(hh + 1) * tm] = ql.T.astype(BF16)
            qt_ref[KV_LORA:KCAT, hh * tm:(hh + 1) * tm] = jnp.where(row_head == hh, qpe_t, 0.0).astype(BF16)
    else:
        qlat_ref, qpe_ref, kcat_ref = mla_refs
        kcat_ref[:, 0:KV_LORA] = ckv
        kcat_ref[:, KV_LORA:KCAT] = kpe
        qpe_ref[...] = qpe
        for hh in range(B_HEADS):
            qlat_ref[hh] = _dot(qn[:, hh * LANES:(hh + 1) * LANES], wuk_ref[hh]) * SM_SCALE


def _proj_in(x, w, cos_tab, sin_tab, tm, prompt):
    T = x.shape[0]
    n_tab = cos_tab.shape[0] // tm
    n_tiles = T // tm
    row = lambda i: (i, 0)
    tab = lambda i: (i % n_tab, 0)
    in_specs = [
        pl.BlockSpec((tm, D_MODEL), row),
        _full_spec((1, D_MODEL)),
        _full_spec(w["wa"].shape), _full_spec(w["wb"].shape), _full_spec(w["wg"].shape),
        _full_spec((1, Q_LORA)), _full_spec((1, KV_LORA)),
        _full_spec(w["wqn"].shape), _full_spec(w["wqp"].shape), _full_spec(w["wuk"].shape),
        pl.BlockSpec((tm, ROPE_T), tab), pl.BlockSpec((tm, ROPE_T), tab),
    ]
    out_shape = [
        jax.ShapeDtypeStruct((T, A_WIDTH), F32),
        jax.ShapeDtypeStruct((T, A_WIDTH), F32),
        jax.ShapeDtypeStruct((T, A_WIDTH), F32),
        jax.ShapeDtypeStruct((T, A_WIDTH), F32),
        jax.ShapeDtypeStruct((T, D_MODEL), BF16),
        jax.ShapeDtypeStruct((T, D_MODEL), BF16),
        jax.ShapeDtypeStruct((T, KV_LORA), F32),
        jax.ShapeDtypeStruct((T, B_ROPE), F32),
    ]
    out_specs = [
        pl.BlockSpec((tm, A_WIDTH), row), pl.BlockSpec((tm, A_WIDTH), row),
        pl.BlockSpec((tm, A_WIDTH), row), pl.BlockSpec((tm, A_WIDTH), row),
        pl.BlockSpec((tm, D_MODEL), row), pl.BlockSpec((tm, D_MODEL), row),
        pl.BlockSpec((tm, KV_LORA), row),
        pl.BlockSpec((tm, B_ROPE), row),
    ]
    tile3 = lambda i: (i, 0, 0)
    if prompt:
        out_shape += [
            jax.ShapeDtypeStruct((n_tiles, KCAT, B_HEADS * tm), BF16),
            jax.ShapeDtypeStruct((n_tiles, tm, KCAT), BF16),
            jax.ShapeDtypeStruct((n_tiles, KV_LORA, tm), BF16),
        ]
        out_specs += [pl.BlockSpec((None, KCAT, B_HEADS * tm), tile3),
                      pl.BlockSpec((None, tm, KCAT), tile3),
                      pl.BlockSpec((None, KV_LORA, tm), tile3)]
    else:
        out_shape += [
            jax.ShapeDtypeStruct((B_HEADS, T, KV_LORA), F32),
            jax.ShapeDtypeStruct((T, ROPE_T), F32),
            jax.ShapeDtypeStruct((T, KCAT), F32),
        ]
        out_specs += [pl.BlockSpec((B_HEADS, tm, KV_LORA), lambda i: (0, i, 0)),
                      pl.BlockSpec((tm, ROPE_T), row),
                      pl.BlockSpec((tm, KCAT), row)]
    return pl.pallas_call(
        functools.partial(_proj_in_kernel, prompt=prompt),
        grid=(n_tiles,), in_specs=in_specs, out_specs=out_specs, out_shape=out_shape,
        compiler_params=_params(("parallel",)),
        name="proj_in_prompt" if prompt else "proj_in_sample",
    )(x, w["g1"], w["wa"], w["wb"], w["wg"], w["qn"], w["kvn"], w["wqn"], w["wqp"], w["wuk"],
      cos_tab, sin_tab)


def _cumsum_rows(g):
    n = g.shape[0]
    row = lax.broadcasted_iota(jnp.int32, g.shape, 0)
    b = g
    shift = 1
    while shift < n:
        b = b + jnp.where(row >= shift, pltpu.roll(b, shift, axis=0), 0.0)
        shift *= 2
    return b


def _hgrn_chunk(q, fr, v, lb, st, chunk, sub):
    n_sub = chunk // sub
    md = _operand_dtype(chunk)
    md_sub = _operand_dtype(sub)
    one_m_lb = 1.0 - lb
    f_gate = lb + one_m_lb * jax.nn.sigmoid(fr)
    g = jnp.log(jnp.maximum(f_gate, F_FLOOR))
    k = one_m_lb * jax.nn.sigmoid(-fr)
    b = _cumsum_rows(g)

    o = _dot_nt((q * jnp.exp(b)).astype(md), st.astype(md))
    b_end = b[chunk - 1:chunk, :]
    kd = (k * jnp.exp(b_end - b)).astype(md)
    st_new = st * jnp.exp(b_end) + _dot(v.T.astype(md), kd)

    if n_sub > 1:
        vs = v.astype(md_sub)
        parts = [jnp.zeros((sub, A_DV), F32)]
        for i in range(1, n_sub):
            lo = i * sub
            r_i = b[lo - 1:lo, :]
            q_i = (q[lo:lo + sub] * jnp.exp(b[lo:lo + sub] - r_i)).astype(md_sub)
            k_i = (k[0:lo] * jnp.exp(r_i - b[0:lo])).astype(md_sub)
            a_i = _dot_nt(q_i, k_i).astype(md_sub)
            parts.append(_dot(a_i, vs[0:lo]))
        o = o + jnp.concatenate(parts, axis=0)

    q3 = q.reshape(n_sub, sub, A_DK)
    k3 = k.reshape(n_sub, sub, A_DK)
    b3 = (b * LOG2E).reshape(n_sub, sub, A_DK)
    v3 = v.astype(md).astype(F32).reshape(n_sub, sub, A_DV)
    tpos = lax.broadcasted_iota(jnp.int32, (n_sub, sub, A_DK), 1)
    od = jnp.zeros((n_sub, sub, A_DV), F32)
    for s in range(sub):
        d = jnp.where(tpos >= s, b3 - b3[:, s:s + 1, :], MASK_VALUE)
        wgt = q3 * k3[:, s:s + 1, :] * jnp.exp2(d)
        od = od + jnp.sum(wgt, axis=-1, keepdims=True) * v3[:, s:s + 1, :]
    return o + od.reshape(chunk, A_DV), st_new


def _hgrn_kernel(*refs, chunk, sub, has_s0):
    if has_s0:
        (qa_ref, fr_ref, v_ref, og_ref, lb_ref, ang_ref, s0_ref, oa_ref, sout_ref, st_ref) = refs
    else:
        (qa_ref, fr_ref, v_ref, og_ref, lb_ref, ang_ref, oa_ref, sout_ref, st_ref) = refs
        s0_ref = None
    j = pl.program_id(1)
    nb, lblk = qa_ref.shape[0], qa_ref.shape[1]
    n_chunks = lblk // chunk

    @pl.when(j == 0)
    def _():
        if has_s0:
            for n in range(nb):
                for hh in range(A_HEADS):
                    st_ref[n, hh] = s0_ref[n, hh].T
        else:
            st_ref[...] = jnp.zeros_like(st_ref)

    def seq_body(n, carry):
        def chunk_body(c, carry2):
            rows = pl.ds(pl.multiple_of(c * chunk, chunk), chunk)
            for hh in range(A_HEADS):
                cols = slice(hh * LANES, (hh + 1) * LANES)
                o, st_new = _hgrn_chunk(qa_ref[n, rows, cols], fr_ref[n, rows, cols], v_ref[n, rows, cols],
                                        lb_ref[:, cols], st_ref[n, hh], chunk, sub)
                st_ref[n, hh] = st_new
                on = _rms(o, ang_ref[:, cols])
                oa_ref[n, rows, cols] = (on * og_ref[n, rows, cols]).astype(oa_ref.dtype)
            return carry2
        return lax.fori_loop(0, n_chunks, chunk_body, carry)
    lax.fori_loop(0, nb, seq_body, 0)

    @pl.when(j == pl.num_programs(1) - 1)
    def _():
        for n in range(nb):
            for hh in range(A_HEADS):
                sout_ref[n, hh] = st_ref[n, hh].T


def _hgrn(qa, fr, v, og, lb, ang, s0, layer, nb, lblk, chunk, sub):
    B, L, _ = qa.shape
    has_s0 = s0 is not None
    grid = (B // nb, L // lblk)
    act = pl.BlockSpec((nb, lblk, A_WIDTH), lambda i, j: (i, j, 0))
    st_spec = pl.BlockSpec((nb, A_HEADS, A_DK, A_DV), lambda i, j: (i, 0, 0, 0))
    in_specs = [act, act, act, act, _full_spec((1, A_WIDTH)), _full_spec((1, A_WIDTH))]
    args = [qa, fr, v, og, lb, ang]
    if has_s0:
        in_specs.append(pl.BlockSpec((None, nb, A_HEADS, A_DK, A_DV), lambda i, j: (layer, i, 0, 0, 0)))
        args.append(s0)
    return pl.pallas_call(
        functools.partial(_hgrn_kernel, chunk=chunk, sub=sub, has_s0=has_s0),
        grid=grid, in_specs=in_specs,
        out_specs=[act, st_spec],
        out_shape=[jax.ShapeDtypeStruct((B, L, A_WIDTH), BF16),
                   jax.ShapeDtypeStruct((B, A_HEADS, A_DK, A_DV), F32)],
        scratch_shapes=[pltpu.VMEM((nb, A_HEADS, A_DV, A_DK), F32)],
        compiler_params=_params(("parallel", "arbitrary")),
        name="hgrn_sample" if has_s0 else "hgrn_prompt",
    )(*args)


def _attn_prompt_kernel(qt_ref, kcat_ref, ckvt_ref, wuvt_ref, out_ref, m_s, l_s, acc_s, *, tq, heads_per_group):
    qi = pl.program_id(1)
    m_s[...] = jnp.full_like(m_s, -jnp.inf)
    l_s[...] = jnp.zeros_like(l_s)
    acc_s[...] = jnp.zeros_like(acc_s)
    gw = heads_per_group * tq

    def tile(kt, diagonal):
        kk = kcat_ref[kt]
        vt = ckvt_ref[kt]
        if diagonal:
            kpos = lax.broadcasted_iota(jnp.int32, (tq, gw), 0)
            qpos = lax.broadcasted_iota(jnp.int32, (tq, gw), 1) % tq
            visible = kpos <= qpos
        for grp in range(B_HEADS // heads_per_group):
            cols = slice(grp * gw, (grp + 1) * gw)
            st = _dot(kk, qt_ref[:, cols])
            if diagonal:
                st = jnp.where(visible, st, MASK_VALUE)
            m_prev = m_s[:, cols]
            m_new = jnp.maximum(m_prev, jnp.max(st, axis=0, keepdims=True))
            alpha = jnp.exp(m_prev - m_new)
            p = jnp.exp(st - m_new)
            l_s[:, cols] = alpha * l_s[:, cols] + jnp.sum(p, axis=0, keepdims=True)
            acc_s[:, cols] = alpha * acc_s[:, cols] + _dot(vt, p.astype(BF16))
            m_s[:, cols] = m_new

    def body(kt, carry):
        tile(kt, False)
        return carry
    lax.fori_loop(0, qi, body, 0)
    tile(qi, True)

    parts = []
    for hh in range(B_HEADS):
        cols = slice(hh * tq, (hh + 1) * tq)
        o_t = (acc_s[:, cols] / l_s[:, cols]).astype(BF16)
        parts.append(_dot(wuvt_ref[hh], o_t))
    out_ref[...] = jnp.concatenate(parts, axis=0).T.astype(out_ref.dtype)


def _attn_prompt(qt, kcat, ckvt, wuvt, B, L, tq, heads_per_group):
    nq = L // tq
    return pl.pallas_call(
        functools.partial(_attn_prompt_kernel, tq=tq, heads_per_group=heads_per_group),
        grid=(B, nq),
        in_specs=[
            pl.BlockSpec((None, KCAT, B_HEADS * tq), lambda b, i: (b * nq + i, 0, 0)),
            pl.BlockSpec((nq, tq, KCAT), lambda b, i: (b, 0, 0)),
            pl.BlockSpec((nq, KV_LORA, tq), lambda b, i: (b, 0, 0)),
            _full_spec(wuvt.shape),
        ],
        out_specs=pl.BlockSpec((tq, B_WIDTH), lambda b, i: (b * nq + i, 0)),
        out_shape=jax.ShapeDtypeStruct((B * L, B_WIDTH), BF16),
        scratch_shapes=[pltpu.VMEM((1, B_HEADS * tq), F32), pltpu.VMEM((1, B_HEADS * tq), F32),
                        pltpu.VMEM((KV_LORA, B_HEADS * tq), F32)],
        compiler_params=_params(("parallel", "arbitrary")),
        name="attn_prompt",
    )(qt, kcat, ckvt, wuvt)


def _lane_tile(x, width):
    if width % LANES == 0:
        return jnp.concatenate([x] * (width // LANES), axis=1) if width > LANES else x
    return x[:, 0:width]


def _attn_sample_kernel(pt_ref, qlat_ref, qpe_ref, knew_ref, wuv_ref, *refs, n_pages, dec):
    ckv_refs = refs[0:n_pages]
    krt_refs = refs[n_pages:2 * n_pages]
    out_ref = refs[2 * n_pages]
    q_s, qp_s, k_s, kpt_s, m_s, l_s, acc_s = refs[2 * n_pages + 1:]
    g = pl.program_id(1)
    rows = B_HEADS * dec

    @pl.when(g == 0)
    def _():
        q_s[...] = qlat_ref[...].reshape(rows, KV_LORA).astype(BF16)
        qpe = qpe_ref[...]
        qp_s[...] = jnp.concatenate(
            [qpe[:, hh * B_ROPE:(hh + 1) * B_ROPE] for hh in range(B_HEADS)], axis=0).astype(BF16)
        m_s[...] = jnp.full_like(m_s, -jnp.inf)
        l_s[...] = jnp.zeros_like(l_s)
        acc_s[...] = jnp.zeros_like(acc_s)

    def update(s, vals):
        n = s.shape[1]
        m_prev = m_s[...]
        m_new = jnp.maximum(m_prev, jnp.max(s, axis=-1, keepdims=True))
        alpha = jnp.exp(m_prev - m_new)
        p = jnp.exp(s - _lane_tile(m_new, n))
        l_s[...] = alpha * l_s[...] + jnp.sum(p, axis=-1, keepdims=True)
        acc_s[...] = _lane_tile(alpha, KV_LORA) * acc_s[...] + _dot(p.astype(vals.dtype), vals)
        m_s[...] = m_new

    for i in range(n_pages):
        k_s[i * PAGE_SIZE:(i + 1) * PAGE_SIZE, :] = ckv_refs[i][...].astype(BF16)
        kpt_s[:, i * PAGE_SIZE:(i + 1) * PAGE_SIZE] = krt_refs[i][...].astype(BF16)
    kc = k_s[...]
    update(_dot_nt(q_s[...], kc) + _dot(qp_s[...], kpt_s[...]), kc)

    @pl.when(g == pl.num_programs(1) - 1)
    def _():
        knew = knew_ref[...]
        kc_new = knew[:, 0:KV_LORA]
        kp_new = knew[:, KV_LORA:KV_LORA + B_ROPE]
        s = _dot_nt(q_s[...].astype(F32), kc_new) + _dot_nt(qp_s[...].astype(F32), kp_new)
        qpos = lax.broadcasted_iota(jnp.int32, (rows, dec), 0) % dec
        kpos = lax.broadcasted_iota(jnp.int32, (rows, dec), 1)
        update(jnp.where(kpos <= qpos, s, MASK_VALUE), kc_new)
        o = (acc_s[...] / _lane_tile(l_s[...], KV_LORA)).astype(_operand_dtype(dec))
        for p in range(B_HEADS // 2):
            pair = jnp.concatenate([o[(2 * p) * dec:(2 * p + 1) * dec],
                                    o[(2 * p + 1) * dec:(2 * p + 2) * dec]], axis=-1)
            out_ref[:, p * LANES:(p + 1) * LANES] = _dot(pair, wuv_ref[p].astype(pair.dtype))


def _attn_sample(page_table, qlat, qpe, kcat, cache_ckv, cache_krope_t, wuv, layer, n_pages):
    n_seq, n_logical = page_table.shape
    dec = qlat.shape[1] // n_seq
    rows = B_HEADS * dec
    groups = n_logical // n_pages

    def page_spec(shape, i):
        return pl.BlockSpec((None, None) + shape, lambda n, g, pt: (layer, pt[n, g * n_pages + i], 0, 0))
    in_specs = [
        pl.BlockSpec((B_HEADS, None, dec, KV_LORA), lambda n, g, pt: (0, n, 0, 0)),
        pl.BlockSpec((None, dec, ROPE_T), lambda n, g, pt: (n, 0, 0)),
        pl.BlockSpec((None, dec, KCAT), lambda n, g, pt: (n, 0, 0)),
        pl.BlockSpec(wuv.shape, lambda n, g, pt: (0, 0, 0)),
    ]
    in_specs += [page_spec((PAGE_SIZE, KV_LORA), i) for i in range(n_pages)]
    in_specs += [page_spec((B_ROPE, PAGE_SIZE), i) for i in range(n_pages)]
    grid_spec = pltpu.PrefetchScalarGridSpec(
        num_scalar_prefetch=1, grid=(n_seq, groups), in_specs=in_specs,
        out_specs=pl.BlockSpec((None, dec, B_WIDTH), lambda n, g, pt: (n, 0, 0)),
        scratch_shapes=[pltpu.VMEM((rows, KV_LORA), BF16), pltpu.VMEM((rows, B_ROPE), BF16),
                        pltpu.VMEM((n_pages * PAGE_SIZE, KV_LORA), BF16),
                        pltpu.VMEM((B_ROPE, n_pages * PAGE_SIZE), BF16),
                        pltpu.VMEM((rows, LANES), F32), pltpu.VMEM((rows, LANES), F32),
                        pltpu.VMEM((rows, KV_LORA), F32)])
    out = pl.pallas_call(
        functools.partial(_attn_sample_kernel, n_pages=n_pages, dec=dec),
        grid_spec=grid_spec,
        out_shape=jax.ShapeDtypeStruct((n_seq, dec, B_WIDTH), F32),
        compiler_params=_params(("parallel", "arbitrary")),
        name="attn_sample",
    )(page_table, qlat.reshape(B_HEADS, n_seq, dec, KV_LORA), qpe.reshape(n_seq, dec, ROPE_T),
      kcat.reshape(n_seq, dec, KCAT), wuv,
      *([cache_ckv] * n_pages), *([cache_krope_t] * n_pages))
    return out.reshape(n_seq * dec, B_WIDTH)


def _post_kernel(x_ref, oa_ref, ob_ref, ga_ref, gb_ref, wbra_ref, wbrb_ref, wout_ref, g2_ref,
                 wup_ref, wdn_ref, gf_ref, out_ref, *, final):
    ma = _dot(oa_ref[...].astype(BF16), wbra_ref[...])
    mb = _dot(ob_ref[...].astype(BF16), wbrb_ref[...])
    merged = ga_ref[...].astype(F32) * ma + gb_ref[...].astype(F32) * mb
    x1 = x_ref[...] + _dot(merged.astype(BF16), wout_ref[...])
    h2 = _rms(x1, g2_ref[...]).astype(BF16)
    acc = x1
    ff_chunk = D_MODEL
    for c in range(D_FF // ff_chunk):
        u = jnp.maximum(_dot(h2, wup_ref[:, c * ff_chunk:(c + 1) * ff_chunk]), 0.0)
        acc = acc + _dot((u * u).astype(BF16), wdn_ref[c * ff_chunk:(c + 1) * ff_chunk, :])
    if final:
        acc = _rms(acc, gf_ref[...])
    out_ref[...] = acc


def _post(x, oa, ob, ga, gb, w, gf, tm, final):
    T = x.shape[0]
    row = lambda i: (i, 0)
    in_specs = [
        pl.BlockSpec((tm, D_MODEL), row), pl.BlockSpec((tm, A_WIDTH), row), pl.BlockSpec((tm, B_WIDTH), row),
        pl.BlockSpec((tm, D_MODEL), row), pl.BlockSpec((tm, D_MODEL), row),
        _full_spec(w["wbra"].shape), _full_spec(w["wbrb"].shape), _full_spec(w["wout"].shape),
        _full_spec((1, D_MODEL)), _full_spec(w["wup"].shape), _full_spec(w["wdn"].shape),
        _full_spec((1, D_MODEL)),
    ]
    return pl.pallas_call(
        functools.partial(_post_kernel, final=final),
        grid=(T // tm,), in_specs=in_specs,
        out_specs=pl.BlockSpec((tm, D_MODEL), row),
        out_shape=jax.ShapeDtypeStruct((T, D_MODEL), F32),
        compiler_params=_params(("parallel",)),
        name="post_final" if final else "post",
    )(x, oa, ob, ga, gb, w["wbra"], w["wbrb"], w["wout"], w["g2"], w["wup"], w["wdn"], gf)


def _rot_half(w):
    half = w.shape[-1] // 2
    return jnp.concatenate([-w[..., half:], w[..., :half]], axis=-1)


def _layer_weights(l, lbs, norm1_g, w_in, a_norm_g, q_norm_g, w_uq, kv_norm_g, w_uk, w_uv,
                   w_br_a, w_br_b, w_out, norm2_g, w_up, w_down):
    wi = w_in[l]
    c_b = 4 * A_WIDTH
    c_rope = c_b + Q_LORA + KV_LORA
    c_g = c_rope + B_ROPE
    rope = wi[:, c_rope:c_g]
    wb = jnp.concatenate([wi[:, c_b:c_rope], jnp.tile(rope, (1, B_HEADS)),
                          jnp.tile(_rot_half(rope), (1, B_HEADS))], axis=-1)
    uq = w_uq[l].reshape(Q_LORA, B_HEADS, B_NOPE + B_ROPE)
    wqn = jnp.pad(uq[:, :, :B_NOPE], ((0, 0), (0, 0), (0, LANES - B_NOPE))).reshape(Q_LORA, B_HEADS * LANES)
    pe = uq[:, :, B_NOPE:]
    wqp = jnp.concatenate([pe.reshape(Q_LORA, ROPE_T), _rot_half(pe).reshape(Q_LORA, ROPE_T)], axis=-1)
    wuk = jnp.pad(jnp.swapaxes(w_uk[l], 1, 2), ((0, 0), (0, LANES - B_NOPE), (0, 0)))
    uv = w_uv[l]
    zeros = jnp.zeros_like(uv[0])
    wuv = jnp.stack([jnp.concatenate([jnp.concatenate([uv[2 * p], zeros], axis=-1),
                                      jnp.concatenate([zeros, uv[2 * p + 1]], axis=-1)], axis=0)
                     for p in range(B_HEADS // 2)])
    bf = lambda a: a.astype(BF16)
    return dict(
        g1=norm1_g[l].reshape(1, D_MODEL), wa=bf(wi[:, :c_b]), wb=bf(wb), wg=bf(wi[:, c_g:]),
        qn=q_norm_g[l].reshape(1, Q_LORA), kvn=kv_norm_g[l].reshape(1, KV_LORA),
        wqn=bf(wqn), wqp=bf(wqp), wuk=bf(wuk), wuv=bf(wuv), wuvt=bf(jnp.swapaxes(uv, 1, 2)),
        lb=lbs[l].reshape(1, A_WIDTH), ang=a_norm_g[l].reshape(1, A_WIDTH),
        wbra=bf(w_br_a[l]), wbrb=bf(w_br_b[l]), wout=bf(w_out[l]),
        g2=norm2_g[l].reshape(1, D_MODEL), wup=bf(w_up[l]), wdn=bf(w_down[l]),
    )


def _rope_tables(pos):
    half = B_ROPE // 2
    inv = ROPE_THETA ** (-jnp.arange(half, dtype=F32) / half)
    ang = pos.astype(F32)[:, None] * inv[None, :]
    cos = jnp.tile(jnp.cos(ang), (1, 2 * B_HEADS))
    sin = jnp.tile(jnp.sin(ang), (1, 2 * B_HEADS))
    return cos, sin


def kernel(x_prompt, x_sample, cache_ckv, cache_krope, state_hgrn, page_table, norm1_g, w_in, lower_bounds,
           a_norm_g, q_norm_g, w_uq, kv_norm_g, w_uk, w_uv, w_br_a, w_br_b, w_out, norm2_g, w_up, w_down,
           final_norm_g):
    B, L, _ = x_prompt.shape
    n_seq, dec, _ = x_sample.shape
    n_logical = page_table.shape[1]
    past_len = n_logical * PAGE_SIZE
    tq = 256
    tm_s = 256
    pages_per_step = min(32, n_logical)

    lbs = _lower_bounds(lower_bounds)
    cos_p, sin_p = _rope_tables(jnp.arange(L, dtype=jnp.int32))
    pos_s = past_len + (jnp.arange(tm_s, dtype=jnp.int32) % dec)
    cos_s, sin_s = _rope_tables(pos_s)
    gf = final_norm_g.reshape(1, D_MODEL)
    cache_krope_t = jnp.swapaxes(cache_krope, 2, 3)

    xp = x_prompt.reshape(B * L, D_MODEL)
    xs = x_sample.reshape(n_seq * dec, D_MODEL)
    outs = [[] for _ in range(6)]
    for l in range(DEPTH):
        w = _layer_weights(l, lbs, norm1_g, w_in, a_norm_g, q_norm_g, w_uq, kv_norm_g, w_uk, w_uv,
                           w_br_a, w_br_b, w_out, norm2_g, w_up, w_down)
        final = l == DEPTH - 1

        qa, fr, v, og, ga, gb, ckv, kpe, qt, kcat, ckvt = _proj_in(xp, w, cos_p, sin_p, tq, True)
        to_seq = lambda a: a.reshape(B, L, A_WIDTH)
        oa, s_p = _hgrn(to_seq(qa), to_seq(fr), to_seq(v), to_seq(og), w["lb"], w["ang"], None, l,
                        nb=1, lblk=256, chunk=64, sub=16)
        ob = _attn_prompt(qt, kcat, ckvt, w["wuvt"], B, L, tq, heads_per_group=8)
        xp = _post(xp, oa.reshape(B * L, A_WIDTH), ob, ga, gb, w, gf, tq, final)
        outs[0].append(ckv.reshape(B, L, KV_LORA))
        outs[1].append(kpe.reshape(B, L, B_ROPE))
        outs[2].append(s_p)

        qa, fr, v, og, ga, gb, ckv, kpe, qlat, qpe, kcat = _proj_in(xs, w, cos_s, sin_s, tm_s, False)
        to_seq = lambda a: a.reshape(n_seq, dec, A_WIDTH)
        oa, s_s = _hgrn(to_seq(qa), to_seq(fr), to_seq(v), to_seq(og), w["lb"], w["ang"], state_hgrn, l,
                        nb=8, lblk=dec, chunk=dec, sub=dec)
        ob = _attn_sample(page_table, qlat, qpe, kcat, cache_ckv, cache_krope_t, w["wuv"], l, pages_per_step)
        xs = _post(xs, oa.reshape(n_seq * dec, A_WIDTH), ob, ga, gb, w, gf, tm_s, final)
        outs[3].append(ckv.reshape(n_seq, dec, KV_LORA))
        outs[4].append(kpe.reshape(n_seq, dec, B_ROPE))
        outs[5].append(s_s)

    return (xp.reshape(B, L, D_MODEL), xs.reshape(n_seq, dec, D_MODEL),
            jnp.stack(outs[0]), jnp.stack(outs[1]), jnp.stack(outs[2]),
            jnp.stack(outs[3]), jnp.stack(outs[4]), jnp.stack(outs[5]))
```

```python
import functools

import jax
import jax.numpy as jnp
from jax import lax
from jax.experimental import pallas as pl
from jax.experimental.pallas import tpu as pltpu

D_MODEL = 1024
DEPTH = 4
PAGE_SIZE = 128
A_HEADS = 4
A_DK = 128
A_DV = 128
A_WIDTH = A_HEADS * A_DV
B_HEADS = 8
B_NOPE = 64
B_ROPE = 32
B_DV = 64
B_WIDTH = B_HEADS * B_DV
Q_LORA = 256
KV_LORA = 256
ROPE_THETA = 10000.0
D_FF = 4 * D_MODEL
EPS = 1e-6
MASK_VALUE = -1e30
F_FLOOR = 1e-30
SM_SCALE = (B_NOPE + B_ROPE) ** -0.5
LOG2E = 1.4426950408889634

LANES = 128
PACKED_ROWS = 16
ROPE_T = B_HEADS * B_ROPE
KCAT = KV_LORA + ROPE_T
VMEM_LIMIT = 56 * 1024 * 1024

BF16 = jnp.bfloat16
F32 = jnp.float32

_NT = (((1,), (1,)), ((), ()))


def _dot(a, b):
    return jnp.dot(a, b, preferred_element_type=F32)


def _dot_nt(a, b):
    return lax.dot_general(a, b, _NT, preferred_element_type=F32)


def _full_spec(shape):
    nd = len(shape)
    return pl.BlockSpec(shape, lambda *_: (0,) * nd)


def _params(sem):
    return pltpu.CompilerParams(dimension_semantics=sem, vmem_limit_bytes=VMEM_LIMIT)


def _rms(x, g):
    ms = jnp.mean(x * x, axis=-1, keepdims=True)
    return x * lax.rsqrt(ms + EPS) * g


def _operand_dtype(rows):
    return BF16 if rows >= PACKED_ROWS else F32


def _lbs_kernel(lb_ref, out_ref):
    x = lb_ref[...]
    m = jnp.max(x, axis=0, keepdims=True)
    e = jnp.exp(x - m)
    p = e / jnp.sum(e, axis=0, keepdims=True)
    acc = jnp.zeros_like(p[0:1])
    out_ref[0:1, :] = acc
    for l in range(1, DEPTH):
        acc = acc + p[l:l + 1]
        out_ref[l:l + 1, :] = acc


def _lower_bounds(lower_bounds):
    return pl.pallas_call(
        _lbs_kernel,
        out_shape=jax.ShapeDtypeStruct(lower_bounds.shape, F32),
        name="hgrn_lower_bounds",
    )(lower_bounds.astype(F32))


def _proj_in_kernel(x_ref, g1_ref, wa_ref, wb_ref, wg_ref, qn_ref, kvn_ref, wqn_ref, wqp_ref,
                    wuk_ref, cos_ref, sin_ref,
                    qa_ref, fr_ref, v_ref, og_ref, ga_ref, gb_ref, ckv_ref, kpe_ref, *mla_refs, prompt):
    tm = x_ref.shape[0]
    h = _rms(x_ref[...], g1_ref[...]).astype(BF16)

    z = _dot(h, wa_ref[:, 0:A_WIDTH])
    qa_ref[...] = z * jax.nn.sigmoid(z)
    fr_ref[...] = _dot(h, wa_ref[:, A_WIDTH:2 * A_WIDTH])
    v_ref[...] = _dot(h, wa_ref[:, 2 * A_WIDTH:3 * A_WIDTH])
    z = _dot(h, wa_ref[:, 3 * A_WIDTH:4 * A_WIDTH])
    og_ref[...] = z * jax.nn.sigmoid(z)

    half = D_MODEL // 2
    for c in range(2):
        ga_ref[:, c * half:(c + 1) * half] = jax.nn.sigmoid(
            _dot(h, wg_ref[:, c * half:(c + 1) * half])).astype(ga_ref.dtype)
        gb_ref[:, c * half:(c + 1) * half] = jax.nn.sigmoid(
            _dot(h, wg_ref[:, D_MODEL + c * half:D_MODEL + (c + 1) * half])).astype(gb_ref.dtype)

    zb = _dot(h, wb_ref[...])
    cos = cos_ref[...]
    sin = sin_ref[...]
    cq = _rms(zb[:, 0:Q_LORA], qn_ref[...]).astype(BF16)
    ckv = _rms(zb[:, Q_LORA:Q_LORA + KV_LORA], kvn_ref[...])
    off = Q_LORA + KV_LORA
    kpe = zb[:, off:off + ROPE_T] * cos + zb[:, off + ROPE_T:off + 2 * ROPE_T] * sin
    ckv_ref[...] = ckv
    kpe_ref[...] = kpe[:, 0:B_ROPE]

    qp2 = _dot(cq, wqp_ref[...])
    qpe = (qp2[:, 0:ROPE_T] * cos + qp2[:, ROPE_T:2 * ROPE_T] * sin) * SM_SCALE
    qn = _dot(cq, wqn_ref[...]).astype(BF16)

    if prompt:
        qt_ref, kcat_ref, ckvt_ref = mla_refs
        kcat_ref[:, 0:KV_LORA] = ckv.astype(BF16)
        kcat_ref[:, KV_LORA:KCAT] = kpe.astype(BF16)
        ckvt_ref[...] = ckv.T.astype(BF16)
        qpe_t = qpe.T
        row_head = lax.broadcasted_iota(jnp.int32, (ROPE_T, tm), 0) // B_ROPE
        for hh in range(B_HEADS):
            ql = _dot(qn[:, hh * LANES:(hh + 1) * LANES], wuk_ref[hh]) * SM_SCALE
            qt_ref[0:KV_LORA, hh * tm:(hh + 1) * tm] = ql.T.astype(BF16)
            qt_ref[KV_LORA:KCAT, hh * tm:(hh + 1) * tm] = jnp.where(row_head == hh, qpe_t, 0.0).astype(BF16)
    else:
        qlat_ref, qpe_ref, kcat_ref = mla_refs
        kcat_ref[:, 0:KV_LORA] = ckv
        kcat_ref[:, KV_LORA:KCAT] = kpe
        qpe_ref[...] = qpe
        for hh in range(B_HEADS):
            qlat_ref[hh] = _dot(qn[:, hh * LANES:(hh + 1) * LANES], wuk_ref[hh]) * SM_SCALE


def _proj_in(x, w, cos_tab, sin_tab, tm, prompt):
    T = x.shape[0]
    n_tab = cos_tab.shape[0] // tm
    n_tiles = T // tm
    row = lambda i: (i, 0)
    tab = lambda i: (i % n_tab, 0)
    in_specs = [
        pl.BlockSpec((tm, D_MODEL), row),
        _full_spec((1, D_MODEL)),
        _full_spec(w["wa"].shape), _full_spec(w["wb"].shape), _full_spec(w["wg"].shape),
        _full_spec((1, Q_LORA)), _full_spec((1, KV_LORA)),
        _full_spec(w["wqn"].shape), _full_spec(w["wqp"].shape), _full_spec(w["wuk"].shape),
        pl.BlockSpec((tm, ROPE_T), tab), pl.BlockSpec((tm, ROPE_T), tab),
    ]
    out_shape = [
        jax.ShapeDtypeStruct((T, A_WIDTH), F32),
        jax.ShapeDtypeStruct((T, A_WIDTH), F32),
        jax.ShapeDtypeStruct((T, A_WIDTH), F32),
        jax.ShapeDtypeStruct((T, A_WIDTH), F32),
        jax.ShapeDtypeStruct((T, D_MODEL), BF16),
        jax.ShapeDtypeStruct((T, D_MODEL), BF16),
        jax.ShapeDtypeStruct((T, KV_LORA), F32),
        jax.ShapeDtypeStruct((T, B_ROPE), F32),
    ]
    out_specs = [
        pl.BlockSpec((tm, A_WIDTH), row), pl.BlockSpec((tm, A_WIDTH), row),
        pl.BlockSpec((tm, A_WIDTH), row), pl.BlockSpec((tm, A_WIDTH), row),
        pl.BlockSpec((tm, D_MODEL), row), pl.BlockSpec((tm, D_MODEL), row),
        pl.BlockSpec((tm, KV_LORA), row),
        pl.BlockSpec((tm, B_ROPE), row),
    ]
    tile3 = lambda i: (i, 0, 0)
    if prompt:
        out_shape += [
            jax.ShapeDtypeStruct((n_tiles, KCAT, B_HEADS * tm), BF16),
            jax.ShapeDtypeStruct((n_tiles, tm, KCAT), BF16),
            jax.ShapeDtypeStruct((n_tiles, KV_LORA, tm), BF16),
        ]
        out_specs += [pl.BlockSpec((None, KCAT, B_HEADS * tm), tile3),
                      pl.BlockSpec((None, tm, KCAT), tile3),
                      pl.BlockSpec((None, KV_LORA, tm), tile3)]
    else:
        out_shape += [
            jax.ShapeDtypeStruct((B_HEADS, T, KV_LORA), F32),
            jax.ShapeDtypeStruct((T, ROPE_T), F32),
            jax.ShapeDtypeStruct((T, KCAT), F32),
        ]
        out_specs += [pl.BlockSpec((B_HEADS, tm, KV_LORA), lambda i: (0, i, 0)),
                      pl.BlockSpec((tm, ROPE_T), row),
                      pl.BlockSpec((tm, KCAT), row)]
    return pl.pallas_call(
        functools.partial(_proj_in_kernel, prompt=prompt),
        grid=(n_tiles,), in_specs=in_specs, out_specs=out_specs, out_shape=out_shape,
        compiler_params=_params(("parallel",)),
        name="proj_in_prompt" if prompt else "proj_in_sample",
    )(x, w["g1"], w["wa"], w["wb"], w["wg"], w["qn"], w["kvn"], w["wqn"], w["wqp"], w["wuk"],
      cos_tab, sin_tab)


def _cumsum_rows(g):
    n = g.shape[0]
    row = lax.broadcasted_iota(jnp.int32, g.shape, 0)
    b = g
    shift = 1
    while shift < n:
        b = b + jnp.where(row >= shift, pltpu.roll(b, shift, axis=0), 0.0)
        shift *= 2
    return b


def _hgrn_chunk(q, fr, v, lb, st, chunk, sub):
    n_sub = chunk // sub
    md = _operand_dtype(chunk)
    md_sub = _operand_dtype(sub)
    one_m_lb = 1.0 - lb
    f_gate = lb + one_m_lb * jax.nn.sigmoid(fr)
    g = jnp.log(jnp.maximum(f_gate, F_FLOOR))
    k = one_m_lb * jax.nn.sigmoid(-fr)
    b = _cumsum_rows(g)

    o = _dot_nt((q * jnp.exp(b)).astype(md), st.astype(md))
    b_end = b[chunk - 1:chunk, :]
    kd = (k * jnp.exp(b_end - b)).astype(md)
    st_new = st * jnp.exp(b_end) + _dot(v.T.astype(md), kd)

    if n_sub > 1:
        vs = v.astype(md_sub)
        parts = [jnp.zeros((sub, A_DV), F32)]
        for i in range(1, n_sub):
            lo = i * sub
            r_i = b[lo - 1:lo, :]
            q_i = (q[lo:lo + sub] * jnp.exp(b[lo:lo + sub] - r_i)).astype(md_sub)
            k_i = (k[0:lo] * jnp.exp(r_i - b[0:lo])).astype(md_sub)
            a_i = _dot_nt(q_i, k_i).astype(md_sub)
            parts.append(_dot(a_i, vs[0:lo]))
        o = o + jnp.concatenate(parts, axis=0)

    q3 = q.reshape(n_sub, sub, A_DK)
    k3 = k.reshape(n_sub, sub, A_DK)
    b3 = (b * LOG2E).reshape(n_sub, sub, A_DK)
    v3 = v.astype(md).astype(F32).reshape(n_sub, sub, A_DV)
    tpos = lax.broadcasted_iota(jnp.int32, (n_sub, sub, A_DK), 1)
    od = jnp.zeros((n_sub, sub, A_DV), F32)
    for s in range(sub):
        d = jnp.where(tpos >= s, b3 - b3[:, s:s + 1, :], MASK_VALUE)
        wgt = q3 * k3[:, s:s + 1, :] * jnp.exp2(d)
        od = od + jnp.sum(wgt, axis=-1, keepdims=True) * v3[:, s:s + 1, :]
    return o + od.reshape(chunk, A_DV), st_new


def _hgrn_kernel(*refs, chunk, sub, has_s0):
    if has_s0:
        (qa_ref, fr_ref, v_ref, og_ref, lb_ref, ang_ref, s0_ref, oa_ref, sout_ref, st_ref) = refs
    else:
        (qa_ref, fr_ref, v_ref, og_ref, lb_ref, ang_ref, oa_ref, sout_ref, st_ref) = refs
        s0_ref = None
    j = pl.program_id(1)
    nb, lblk = qa_ref.shape[0], qa_ref.shape[1]
    n_chunks = lblk // chunk

    @pl.when(j == 0)
    def _():
        if has_s0:
            for n in range(nb):
                for hh in range(A_HEADS):
                    st_ref[n, hh] = s0_ref[n, hh].T
        else:
            st_ref[...] = jnp.zeros_like(st_ref)

    def seq_body(n, carry):
        def chunk_body(c, carry2):
            rows = pl.ds(pl.multiple_of(c * chunk, chunk), chunk)
            for hh in range(A_HEADS):
                cols = slice(hh * LANES, (hh + 1) * LANES)
                o, st_new = _hgrn_chunk(qa_ref[n, rows, cols], fr_ref[n, rows, cols], v_ref[n, rows, cols],
                                        lb_ref[:, cols], st_ref[n, hh], chunk, sub)
                st_ref[n, hh] = st_new
                on = _rms(o, ang_ref[:, cols])
                oa_ref[n, rows, cols] = (on * og_ref[n, rows, cols]).astype(oa_ref.dtype)
            return carry2
        return lax.fori_loop(0, n_chunks, chunk_body, carry, unroll=min(4, n_chunks))
    lax.fori_loop(0, nb, seq_body, 0)

    @pl.when(j == pl.num_programs(1) - 1)
    def _():
        for n in range(nb):
            for hh in range(A_HEADS):
                sout_ref[n, hh] = st_ref[n, hh].T


def _hgrn(qa, fr, v, og, lb, ang, s0, layer, nb, lblk, chunk, sub):
    B, L, _ = qa.shape
    has_s0 = s0 is not None
    grid = (B // nb, L // lblk)
    act = pl.BlockSpec((nb, lblk, A_WIDTH), lambda i, j: (i, j, 0))
    st_spec = pl.BlockSpec((nb, A_HEADS, A_DK, A_DV), lambda i, j: (i, 0, 0, 0))
    in_specs = [act, act, act, act, _full_spec((1, A_WIDTH)), _full_spec((1, A_WIDTH))]
    args = [qa, fr, v, og, lb, ang]
    if has_s0:
        in_specs.append(pl.BlockSpec((None, nb, A_HEADS, A_DK, A_DV), lambda i, j: (layer, i, 0, 0, 0)))
        args.append(s0)
    return pl.pallas_call(
        functools.partial(_hgrn_kernel, chunk=chunk, sub=sub, has_s0=has_s0),
        grid=grid, in_specs=in_specs,
        out_specs=[act, st_spec],
        out_shape=[jax.ShapeDtypeStruct((B, L, A_WIDTH), BF16),
                   jax.ShapeDtypeStruct((B, A_HEADS, A_DK, A_DV), F32)],
        scratch_shapes=[pltpu.VMEM((nb, A_HEADS, A_DV, A_DK), F32)],
        compiler_params=_params(("parallel", "arbitrary")),
        name="hgrn_sample" if has_s0 else "hgrn_prompt",
    )(*args)


def _attn_prompt_kernel(qt_ref, kcat_ref, ckvt_ref, wuvt_ref, out_ref, m_s, l_s, acc_s, *, tq, heads_per_group):
    qi = pl.program_id(1)
    m_s[...] = jnp.full_like(m_s, -jnp.inf)
    l_s[...] = jnp.zeros_like(l_s)
    acc_s[...] = jnp.zeros_like(acc_s)
    gw = heads_per_group * tq

    def tile(kt, diagonal):
        kk = kcat_ref[kt]
        vt = ckvt_ref[kt]
        if diagonal:
            kpos = lax.broadcasted_iota(jnp.int32, (tq, gw), 0)
            qpos = lax.broadcasted_iota(jnp.int32, (tq, gw), 1) % tq
            visible = kpos <= qpos
        for grp in range(B_HEADS // heads_per_group):
            cols = slice(grp * gw, (grp + 1) * gw)
            st = _dot(kk, qt_ref[:, cols])
            if diagonal:
                st = jnp.where(visible, st, MASK_VALUE)
            m_prev = m_s[:, cols]
            m_new = jnp.maximum(m_prev, jnp.max(st, axis=0, keepdims=True))
            alpha = jnp.exp(m_prev - m_new)
            p = jnp.exp(st - m_new)
            l_s[:, cols] = alpha * l_s[:, cols] + jnp.sum(p, axis=0, keepdims=True)
            acc_s[:, cols] = alpha * acc_s[:, cols] + _dot(vt, p.astype(BF16))
            m_s[:, cols] = m_new

    def body(kt, carry):
        tile(kt, False)
        return carry
    lax.fori_loop(0, qi, body, 0)
    tile(qi, True)

    parts = []
    for hh in range(B_HEADS):
        cols = slice(hh * tq, (hh + 1) * tq)
        o_t = (acc_s[:, cols] / l_s[:, cols]).astype(BF16)
        parts.append(_dot(wuvt_ref[hh], o_t))
    out_ref[...] = jnp.concatenate(parts, axis=0).T.astype(out_ref.dtype)


def _attn_prompt(qt, kcat, ckvt, wuvt, B, L, tq, heads_per_group):
    nq = L // tq
    return pl.pallas_call(
        functools.partial(_attn_prompt_kernel, tq=tq, heads_per_group=heads_per_group),
        grid=(B, nq),
        in_specs=[
            pl.BlockSpec((None, KCAT, B_HEADS * tq), lambda b, i: (b * nq + i, 0, 0)),
            pl.BlockSpec((nq, tq, KCAT), lambda b, i: (b, 0, 0)),
            pl.BlockSpec((nq, KV_LORA, tq), lambda b, i: (b, 0, 0)),
            _full_spec(wuvt.shape),
        ],
        out_specs=pl.BlockSpec((tq, B_WIDTH), lambda b, i: (b * nq + i, 0)),
        out_shape=jax.ShapeDtypeStruct((B * L, B_WIDTH), BF16),
        scratch_shapes=[pltpu.VMEM((1, B_HEADS * tq), F32), pltpu.VMEM((1, B_HEADS * tq), F32),
                        pltpu.VMEM((KV_LORA, B_HEADS * tq), F32)],
        compiler_params=_params(("parallel", "arbitrary")),
        name="attn_prompt",
    )(qt, kcat, ckvt, wuvt)


def _lane_tile(x, width):
    if width % LANES == 0:
        return jnp.concatenate([x] * (width // LANES), axis=1) if width > LANES else x
    return x[:, 0:width]


def _attn_sample_kernel(pt_ref, qlat_ref, qpe_ref, knew_ref, wuv_ref, ckv_hbm, krt_hbm, out_ref,
                        q_s, qp_s, k_s, kt_s, kpt_s, m_s, l_s, acc_s, kbuf, krbuf, sem,
                        *, layer, n_pages, n_seq_blk, dec):
    blk = pl.program_id(0)
    g = pl.program_id(1)
    n_groups = pl.num_programs(1)
    step = blk * n_groups + g
    n_steps = pl.num_programs(0) * n_groups
    slot = lax.rem(step, 2)
    rows = B_HEADS * dec

    def page_copies(slot_, page_of):
        out = []
        for sq in range(n_seq_blk):
            for i in range(n_pages):
                page = page_of(sq, i)
                keys = pl.ds(i * PAGE_SIZE, PAGE_SIZE)
                out.append(pltpu.make_async_copy(ckv_hbm.at[layer, page], kbuf.at[slot_, sq, keys, :], sem.at[slot_]))
                out.append(pltpu.make_async_copy(krt_hbm.at[layer, page], krbuf.at[slot_, sq, :, keys], sem.at[slot_]))
        return out

    def start_step(step_, slot_):
        blk_ = step_ // n_groups
        g_ = step_ - blk_ * n_groups
        for c in page_copies(slot_, lambda sq, i: pt_ref[blk_ * n_seq_blk + sq, g_ * n_pages + i]):
            c.start()

    @pl.when(step == 0)
    def _():
        start_step(step, slot)

    for c in page_copies(slot, lambda sq, i: 0):
        c.wait()

    @pl.when(step + 1 < n_steps)
    def _():
        start_step(step + 1, 1 - slot)

    @pl.when(g == 0)
    def _():
        for sq in range(n_seq_blk):
            q_s[sq] = qlat_ref[:, sq].reshape(rows, KV_LORA).astype(BF16)
            qpe = qpe_ref[sq]
            qp_s[sq] = jnp.concatenate(
                [qpe[:, hh * B_ROPE:(hh + 1) * B_ROPE] for hh in range(B_HEADS)], axis=0).astype(BF16)
        m_s[...] = jnp.full_like(m_s, -jnp.inf)
        l_s[...] = jnp.zeros_like(l_s)
        acc_s[...] = jnp.zeros_like(acc_s)

    def update(sq, s, vals):
        n = s.shape[1]
        m_prev = m_s[sq]
        m_new = jnp.maximum(m_prev, jnp.max(s, axis=-1, keepdims=True))
        alpha = jnp.exp(m_prev - m_new)
        p = jnp.exp(s - _lane_tile(m_new, n))
        l_s[sq] = alpha * l_s[sq] + jnp.sum(p, axis=-1, keepdims=True)
        acc_s[sq] = _lane_tile(alpha, KV_LORA) * acc_s[sq] + _dot(p.astype(vals.dtype), vals)
        m_s[sq] = m_new

    for sq in range(n_seq_blk):
        for i in range(n_pages):
            page = kbuf[slot, sq, i * PAGE_SIZE:(i + 1) * PAGE_SIZE, :]
            k_s[sq, i * PAGE_SIZE:(i + 1) * PAGE_SIZE, :] = page.astype(BF16)
            kt_s[sq, :, i * PAGE_SIZE:(i + 1) * PAGE_SIZE] = page.T.astype(BF16)
        kpt_s[sq] = krbuf[slot, sq].astype(BF16)
        kc = k_s[sq]
        update(sq, _dot(q_s[sq], kt_s[sq]) + _dot(qp_s[sq], kpt_s[sq]), kc)

    @pl.when(g == pl.num_programs(1) - 1)
    def _():
        qpos = lax.broadcasted_iota(jnp.int32, (rows, dec), 0) % dec
        kpos = lax.broadcasted_iota(jnp.int32, (rows, dec), 1)
        for sq in range(n_seq_blk):
            knew = knew_ref[sq]
            kc_new = knew[:, 0:KV_LORA]
            kp_new = knew[:, KV_LORA:KV_LORA + B_ROPE]
            s = _dot_nt(q_s[sq].astype(F32), kc_new) + _dot_nt(qp_s[sq].astype(F32), kp_new)
            update(sq, jnp.where(kpos <= qpos, s, MASK_VALUE), kc_new)
            o = (acc_s[sq] / _lane_tile(l_s[sq], KV_LORA)).astype(_operand_dtype(dec))
            for p in range(B_HEADS // 2):
                pair = jnp.concatenate([o[(2 * p) * dec:(2 * p + 1) * dec],
                                        o[(2 * p + 1) * dec:(2 * p + 2) * dec]], axis=-1)
                out_ref[sq, :, p * LANES:(p + 1) * LANES] = _dot(pair, wuv_ref[p].astype(pair.dtype))


def _attn_sample(page_table, qlat, qpe, kcat, cache_ckv, cache_krope_t, wuv, layer, n_pages, n_seq_blk):
    n_seq, n_logical = page_table.shape
    dec = qlat.shape[1] // n_seq
    rows = B_HEADS * dec
    groups = n_logical // n_pages

    seq3 = lambda n, g, pt: (n, 0, 0)
    in_specs = [
        pl.BlockSpec((B_HEADS, n_seq_blk, dec, KV_LORA), lambda n, g, pt: (0, n, 0, 0)),
        pl.BlockSpec((n_seq_blk, dec, ROPE_T), seq3),
        pl.BlockSpec((n_seq_blk, dec, KCAT), seq3),
        pl.BlockSpec(wuv.shape, lambda n, g, pt: (0, 0, 0)),
        pl.BlockSpec(memory_space=pl.ANY),
        pl.BlockSpec(memory_space=pl.ANY),
    ]
    keys = n_pages * PAGE_SIZE
    grid_spec = pltpu.PrefetchScalarGridSpec(
        num_scalar_prefetch=1, grid=(n_seq // n_seq_blk, groups), in_specs=in_specs,
        out_specs=pl.BlockSpec((n_seq_blk, dec, B_WIDTH), seq3),
        scratch_shapes=[pltpu.VMEM((n_seq_blk, rows, KV_LORA), BF16), pltpu.VMEM((n_seq_blk, rows, B_ROPE), BF16),
                        pltpu.VMEM((n_seq_blk, keys, KV_LORA), BF16),
                        pltpu.VMEM((n_seq_blk, KV_LORA, keys), BF16),
                        pltpu.VMEM((n_seq_blk, B_ROPE, keys), BF16),
                        pltpu.VMEM((n_seq_blk, rows, LANES), F32), pltpu.VMEM((n_seq_blk, rows, LANES), F32),
                        pltpu.VMEM((n_seq_blk, rows, KV_LORA), F32),
                        pltpu.VMEM((2, n_seq_blk, keys, KV_LORA), F32),
                        pltpu.VMEM((2, n_seq_blk, B_ROPE, keys), F32),
                        pltpu.SemaphoreType.DMA((2,))])
    out = pl.pallas_call(
        functools.partial(_attn_sample_kernel, layer=layer, n_pages=n_pages, n_seq_blk=n_seq_blk, dec=dec),
        grid_spec=grid_spec,
        out_shape=jax.ShapeDtypeStruct((n_seq, dec, B_WIDTH), F32),
        compiler_params=_params(("arbitrary", "arbitrary")),
        name="attn_sample",
    )(page_table, qlat.reshape(B_HEADS, n_seq, dec, KV_LORA), qpe.reshape(n_seq, dec, ROPE_T),
      kcat.reshape(n_seq, dec, KCAT), wuv, cache_ckv, cache_krope_t)
    return out.reshape(n_seq * dec, B_WIDTH)


def _post_kernel(x_ref, oa_ref, ob_ref, ga_ref, gb_ref, wbra_ref, wbrb_ref, wout_ref, g2_ref,
                 wup_ref, wdn_ref, gf_ref, out_ref, *, final):
    ma = _dot(oa_ref[...].astype(BF16), wbra_ref[...])
    mb = _dot(ob_ref[...].astype(BF16), wbrb_ref[...])
    merged = ga_ref[...].astype(F32) * ma + gb_ref[...].astype(F32) * mb
    x1 = x_ref[...] + _dot(merged.astype(BF16), wout_ref[...])
    h2 = _rms(x1, g2_ref[...]).astype(BF16)
    acc = x1
    ff_chunk = D_MODEL
    for c in range(D_FF // ff_chunk):
        u = jnp.maximum(_dot(h2, wup_ref[:, c * ff_chunk:(c + 1) * ff_chunk]), 0.0)
        acc = acc + _dot((u * u).astype(BF16), wdn_ref[c * ff_chunk:(c + 1) * ff_chunk, :])
    if final:
        acc = _rms(acc, gf_ref[...])
    out_ref[...] = acc


def _post(x, oa, ob, ga, gb, w, gf, tm, final):
    T = x.shape[0]
    row = lambda i: (i, 0)
    in_specs = [
        pl.BlockSpec((tm, D_MODEL), row), pl.BlockSpec((tm, A_WIDTH), row), pl.BlockSpec((tm, B_WIDTH), row),
        pl.BlockSpec((tm, D_MODEL), row), pl.BlockSpec((tm, D_MODEL), row),
        _full_spec(w["wbra"].shape), _full_spec(w["wbrb"].shape), _full_spec(w["wout"].shape),
        _full_spec((1, D_MODEL)), _full_spec(w["wup"].shape), _full_spec(w["wdn"].shape),
        _full_spec((1, D_MODEL)),
    ]
    return pl.pallas_call(
        functools.partial(_post_kernel, final=final),
        grid=(T // tm,), in_specs=in_specs,
        out_specs=pl.BlockSpec((tm, D_MODEL), row),
        out_shape=jax.ShapeDtypeStruct((T, D_MODEL), F32),
        compiler_params=_params(("parallel",)),
        name="post_final" if final else "post",
    )(x, oa, ob, ga, gb, w["wbra"], w["wbrb"], w["wout"], w["g2"], w["wup"], w["wdn"], gf)


def _rot_half(w):
    half = w.shape[-1] // 2
    return jnp.concatenate([-w[..., half:], w[..., :half]], axis=-1)


def _layer_weights(l, lbs, norm1_g, w_in, a_norm_g, q_norm_g, w_uq, kv_norm_g, w_uk, w_uv,
                   w_br_a, w_br_b, w_out, norm2_g, w_up, w_down):
    wi = w_in[l]
    c_b = 4 * A_WIDTH
    c_rope = c_b + Q_LORA + KV_LORA
    c_g = c_rope + B_ROPE
    rope = wi[:, c_rope:c_g]
    wb = jnp.concatenate([wi[:, c_b:c_rope], jnp.tile(rope, (1, B_HEADS)),
                          jnp.tile(_rot_half(rope), (1, B_HEADS))], axis=-1)
    uq = w_uq[l].reshape(Q_LORA, B_HEADS, B_NOPE + B_ROPE)
    wqn = jnp.pad(uq[:, :, :B_NOPE], ((0, 0), (0, 0), (0, LANES - B_NOPE))).reshape(Q_LORA, B_HEADS * LANES)
    pe = uq[:, :, B_NOPE:]
    wqp = jnp.concatenate([pe.reshape(Q_LORA, ROPE_T), _rot_half(pe).reshape(Q_LORA, ROPE_T)], axis=-1)
    wuk = jnp.pad(jnp.swapaxes(w_uk[l], 1, 2), ((0, 0), (0, LANES - B_NOPE), (0, 0)))
    uv = w_uv[l]
    zeros = jnp.zeros_like(uv[0])
    wuv = jnp.stack([jnp.concatenate([jnp.concatenate([uv[2 * p], zeros], axis=-1),
                                      jnp.concatenate([zeros, uv[2 * p + 1]], axis=-1)], axis=0)
                     for p in range(B_HEADS // 2)])
    bf = lambda a: a.astype(BF16)
    return dict(
        g1=norm1_g[l].reshape(1, D_MODEL), wa=bf(wi[:, :c_b]), wb=bf(wb), wg=bf(wi[:, c_g:]),
        qn=q_norm_g[l].reshape(1, Q_LORA), kvn=kv_norm_g[l].reshape(1, KV_LORA),
        wqn=bf(wqn), wqp=bf(wqp), wuk=bf(wuk), wuv=bf(wuv), wuvt=bf(jnp.swapaxes(uv, 1, 2)),
        lb=lbs[l].reshape(1, A_WIDTH), ang=a_norm_g[l].reshape(1, A_WIDTH),
        wbra=bf(w_br_a[l]), wbrb=bf(w_br_b[l]), wout=bf(w_out[l]),
        g2=norm2_g[l].reshape(1, D_MODEL), wup=bf(w_up[l]), wdn=bf(w_down[l]),
    )


def _rope_tables(pos):
    half = B_ROPE // 2
    inv = ROPE_THETA ** (-jnp.arange(half, dtype=F32) / half)
    ang = pos.astype(F32)[:, None] * inv[None, :]
    cos = jnp.tile(jnp.cos(ang), (1, 2 * B_HEADS))
    sin = jnp.tile(jnp.sin(ang), (1, 2 * B_HEADS))
    return cos, sin


def kernel(x_prompt, x_sample, cache_ckv, cache_krope, state_hgrn, page_table, norm1_g, w_in, lower_bounds,
           a_norm_g, q_norm_g, w_uq, kv_norm_g, w_uk, w_uv, w_br_a, w_br_b, w_out, norm2_g, w_up, w_down,
           final_norm_g):
    B, L, _ = x_prompt.shape
    n_seq, dec, _ = x_sample.shape
    n_logical = page_table.shape[1]
    past_len = n_logical * PAGE_SIZE
    tq = 256
    tm_s = 256
    pages_per_step = min(16, n_logical)
    seqs_per_step = 2

    lbs = _lower_bounds(lower_bounds)
    cos_p, sin_p = _rope_tables(jnp.arange(L, dtype=jnp.int32))
    pos_s = past_len + (jnp.arange(tm_s, dtype=jnp.int32) % dec)
    cos_s, sin_s = _rope_tables(pos_s)
    gf = final_norm_g.reshape(1, D_MODEL)
    cache_krope_t = jnp.swapaxes(cache_krope, 2, 3)

    xp = x_prompt.reshape(B * L, D_MODEL)
    xs = x_sample.reshape(n_seq * dec, D_MODEL)
    outs = [[] for _ in range(6)]
    for l in range(DEPTH):
        w = _layer_weights(l, lbs, norm1_g, w_in, a_norm_g, q_norm_g, w_uq, kv_norm_g, w_uk, w_uv,
                           w_br_a, w_br_b, w_out, norm2_g, w_up, w_down)
        final = l == DEPTH - 1

        qa, fr, v, og, ga, gb, ckv, kpe, qt, kcat, ckvt = _proj_in(xp, w, cos_p, sin_p, tq, True)
        to_seq = lambda a: a.reshape(B, L, A_WIDTH)
        oa, s_p = _hgrn(to_seq(qa), to_seq(fr), to_seq(v), to_seq(og), w["lb"], w["ang"], None, l,
                        nb=1, lblk=256, chunk=64, sub=16)
        ob = _attn_prompt(qt, kcat, ckvt, w["wuvt"], B, L, tq, heads_per_group=8)
        xp = _post(xp, oa.reshape(B * L, A_WIDTH), ob, ga, gb, w, gf, tq, final)
        outs[0].append(ckv.reshape(B, L, KV_LORA))
        outs[1].append(kpe.reshape(B, L, B_ROPE))
        outs[2].append(s_p)

        qa, fr, v, og, ga, gb, ckv, kpe, qlat, qpe, kcat = _proj_in(xs, w, cos_s, sin_s, tm_s, False)
        to_seq = lambda a: a.reshape(n_seq, dec, A_WIDTH)
        oa, s_s = _hgrn(to_seq(qa), to_seq(fr), to_seq(v), to_seq(og), w["lb"], w["ang"], state_hgrn, l,
                        nb=8, lblk=dec, chunk=dec, sub=dec)
        ob = _attn_sample(page_table, qlat, qpe, kcat, cache_ckv, cache_krope_t, w["wuv"], l, pages_per_step,
                          seqs_per_step)
        xs = _post(xs, oa.reshape(n_seq * dec, A_WIDTH), ob, ga, gb, w, gf, tm_s, final)
        outs[3].append(ckv.reshape(n_seq, dec, KV_LORA))
        outs[4].append(kpe.reshape(n_seq, dec, B_ROPE))
        outs[5].append(s_s)

    return (xp.reshape(B, L, D_MODEL), xs.reshape(n_seq, dec, D_MODEL),
            jnp.stack(outs[0]), jnp.stack(outs[1]), jnp.stack(outs[2]),
            jnp.stack(outs[3]), jnp.stack(outs[4]), jnp.stack(outs[5]))
```

```python
import functools

import jax
import jax.numpy as jnp
from jax import lax
from jax.experimental import pallas as pl
from jax.experimental.pallas import tpu as pltpu

D_MODEL = 1024
DEPTH = 4
PAGE_SIZE = 128
A_HEADS = 4
A_DK = 128
A_DV = 128
A_WIDTH = A_HEADS * A_DV
B_HEADS = 8
B_NOPE = 64
B_ROPE = 32
B_DV = 64
B_WIDTH = B_HEADS * B_DV
Q_LORA = 256
KV_LORA = 256
ROPE_THETA = 10000.0
D_FF = 4 * D_MODEL
EPS = 1e-6
MASK_VALUE = -1e30
F_FLOOR = 1e-30
SM_SCALE = (B_NOPE + B_ROPE) ** -0.5
LOG2E = 1.4426950408889634

LANES = 128
PACKED_ROWS = 16
ROPE_T = B_HEADS * B_ROPE
KCAT = KV_LORA + ROPE_T
VMEM_LIMIT = 56 * 1024 * 1024

BF16 = jnp.bfloat16
F32 = jnp.float32

_NT = (((1,), (1,)), ((), ()))


def _dot(a, b):
    return jnp.dot(a, b, preferred_element_type=F32)


def _dot_nt(a, b):
    return lax.dot_general(a, b, _NT, preferred_element_type=F32)


def _full_spec(shape):
    nd = len(shape)
    return pl.BlockSpec(shape, lambda *_: (0,) * nd)


def _params(sem):
    return pltpu.CompilerParams(dimension_semantics=sem, vmem_limit_bytes=VMEM_LIMIT)


def _rms(x, g):
    ms = jnp.mean(x * x, axis=-1, keepdims=True)
    return x * lax.rsqrt(ms + EPS) * g


def _operand_dtype(rows):
    return BF16 if rows >= PACKED_ROWS else F32


def _lbs_kernel(lb_ref, out_ref):
    x = lb_ref[...]
    m = jnp.max(x, axis=0, keepdims=True)
    e = jnp.exp(x - m)
    p = e / jnp.sum(e, axis=0, keepdims=True)
    acc = jnp.zeros_like(p[0:1])
    out_ref[0:1, :] = acc
    for l in range(1, DEPTH):
        acc = acc + p[l:l + 1]
        out_ref[l:l + 1, :] = acc


def _lower_bounds(lower_bounds):
    return pl.pallas_call(
        _lbs_kernel,
        out_shape=jax.ShapeDtypeStruct(lower_bounds.shape, F32),
        name="hgrn_lower_bounds",
    )(lower_bounds.astype(F32))


def _mixer_a_and_gates(h, wa_ref, wg_ref, qa_ref, fr_ref, v_ref, og_ref, ga_ref, gb_ref):
    z = _dot(h, wa_ref[:, 0:A_WIDTH])
    qa_ref[...] = z * jax.nn.sigmoid(z)
    fr_ref[...] = _dot(h, wa_ref[:, A_WIDTH:2 * A_WIDTH])
    v_ref[...] = _dot(h, wa_ref[:, 2 * A_WIDTH:3 * A_WIDTH])
    z = _dot(h, wa_ref[:, 3 * A_WIDTH:4 * A_WIDTH])
    og_ref[...] = z * jax.nn.sigmoid(z)
    half = D_MODEL // 2
    for c in range(2):
        ga_ref[:, c * half:(c + 1) * half] = jax.nn.sigmoid(
            _dot(h, wg_ref[:, c * half:(c + 1) * half])).astype(ga_ref.dtype)
        gb_ref[:, c * half:(c + 1) * half] = jax.nn.sigmoid(
            _dot(h, wg_ref[:, D_MODEL + c * half:D_MODEL + (c + 1) * half])).astype(gb_ref.dtype)


def _proj_in_sample_kernel(x_ref, g1_ref, wa_ref, wb_ref, wg_ref, qn_ref, kvn_ref, wqn_ref, wqp_ref,
                           wuk_ref, cos_ref, sin_ref,
                           qa_ref, fr_ref, v_ref, og_ref, ga_ref, gb_ref, ckv_ref, kpe_ref,
                           qlat_ref, qpe_ref, kcat_ref):
    h = _rms(x_ref[...], g1_ref[...]).astype(BF16)
    _mixer_a_and_gates(h, wa_ref, wg_ref, qa_ref, fr_ref, v_ref, og_ref, ga_ref, gb_ref)

    zb = _dot(h, wb_ref[...])
    cos = cos_ref[...]
    sin = sin_ref[...]
    cq = _rms(zb[:, 0:Q_LORA], qn_ref[...]).astype(BF16)
    ckv = _rms(zb[:, Q_LORA:Q_LORA + KV_LORA], kvn_ref[...])
    off = Q_LORA + KV_LORA
    kpe = zb[:, off:off + ROPE_T] * cos + zb[:, off + ROPE_T:off + 2 * ROPE_T] * sin
    ckv_ref[...] = ckv
    kpe_ref[...] = kpe[:, 0:B_ROPE]
    kcat_ref[:, 0:KV_LORA] = ckv
    kcat_ref[:, KV_LORA:KCAT] = kpe

    qp2 = _dot(cq, wqp_ref[...])
    qpe_ref[...] = (qp2[:, 0:ROPE_T] * cos + qp2[:, ROPE_T:2 * ROPE_T] * sin) * SM_SCALE
    qn = _dot(cq, wqn_ref[...]).astype(BF16)
    for hh in range(B_HEADS):
        qlat_ref[hh] = _dot(qn[:, hh * LANES:(hh + 1) * LANES], wuk_ref[hh]) * SM_SCALE


def _proj_in_prompt_kernel(x_ref, g1_ref, wa_ref, wb_ref, wg_ref, qn_ref, kvn_ref, wqf_ref, wqr_ref,
                           wukf_ref, wuvf_ref, cos_ref, sin_ref,
                           qa_ref, fr_ref, v_ref, og_ref, ga_ref, gb_ref, ckv_ref, kpe_ref,
                           qt_ref, kh_ref, vt_ref):
    h = _rms(x_ref[...], g1_ref[...]).astype(BF16)
    _mixer_a_and_gates(h, wa_ref, wg_ref, qa_ref, fr_ref, v_ref, og_ref, ga_ref, gb_ref)

    zb = _dot(h, wb_ref[...])
    cos = cos_ref[...]
    sin = sin_ref[...]
    cq = _rms(zb[:, 0:Q_LORA], qn_ref[...]).astype(BF16)
    ckv = _rms(zb[:, Q_LORA:Q_LORA + KV_LORA], kvn_ref[...])
    off = Q_LORA + KV_LORA
    kpe = zb[:, off:off + LANES] * cos + zb[:, off + LANES:off + 2 * LANES] * sin
    ckv_ref[...] = ckv
    kpe_ref[...] = kpe[:, 0:B_ROPE]

    kn = _dot(ckv.astype(BF16), wukf_ref[...])
    vt_ref[...] = _dot(wuvf_ref[...], ckv.T.astype(BF16)).astype(BF16)
    zq = _dot(cq, wqf_ref[...])
    zr = _dot(cq, wqr_ref[...])
    for hh in range(B_HEADS):
        blk = slice(hh * LANES, (hh + 1) * LANES)
        kh_ref[hh] = (kn[:, blk] + kpe).astype(BF16)
        qf = (zq[:, blk] * cos + zr[:, blk] * sin) * SM_SCALE
        qt_ref[hh] = qf.T.astype(BF16)


def _proj_in(x, w, cos_tab, sin_tab, tm, prompt):
    T = x.shape[0]
    n_tab = cos_tab.shape[0] // tm
    tab_w = cos_tab.shape[1]
    n_tiles = T // tm
    row = lambda i: (i, 0)
    tab = lambda i: (i % n_tab, 0)
    if prompt:
        weights = [w["wa"], w["wbp"], w["wg"], w["qn"], w["kvn"], w["wqf"], w["wqr"], w["wukf"], w["wuvf"]]
    else:
        weights = [w["wa"], w["wb"], w["wg"], w["qn"], w["kvn"], w["wqn"], w["wqp"], w["wuk"]]
    in_specs = ([pl.BlockSpec((tm, D_MODEL), row), _full_spec((1, D_MODEL))]
                + [_full_spec(a.shape) for a in weights]
                + [pl.BlockSpec((tm, tab_w), tab), pl.BlockSpec((tm, tab_w), tab)])
    out_shape = [
        jax.ShapeDtypeStruct((T, A_WIDTH), F32),
        jax.ShapeDtypeStruct((T, A_WIDTH), F32),
        jax.ShapeDtypeStruct((T, A_WIDTH), F32),
        jax.ShapeDtypeStruct((T, A_WIDTH), F32),
        jax.ShapeDtypeStruct((T, D_MODEL), BF16),
        jax.ShapeDtypeStruct((T, D_MODEL), BF16),
        jax.ShapeDtypeStruct((T, KV_LORA), F32),
        jax.ShapeDtypeStruct((T, B_ROPE), F32),
    ]
    out_specs = [
        pl.BlockSpec((tm, A_WIDTH), row), pl.BlockSpec((tm, A_WIDTH), row),
        pl.BlockSpec((tm, A_WIDTH), row), pl.BlockSpec((tm, A_WIDTH), row),
        pl.BlockSpec((tm, D_MODEL), row), pl.BlockSpec((tm, D_MODEL), row),
        pl.BlockSpec((tm, KV_LORA), row),
        pl.BlockSpec((tm, B_ROPE), row),
    ]
    if prompt:
        tile4 = lambda i: (i, 0, 0, 0)
        out_shape += [
            jax.ShapeDtypeStruct((n_tiles, B_HEADS, LANES, tm), BF16),
            jax.ShapeDtypeStruct((n_tiles, B_HEADS, tm, LANES), BF16),
            jax.ShapeDtypeStruct((n_tiles, B_WIDTH, tm), BF16),
        ]
        out_specs += [pl.BlockSpec((None, B_HEADS, LANES, tm), tile4),
                      pl.BlockSpec((None, B_HEADS, tm, LANES), tile4),
                      pl.BlockSpec((None, B_WIDTH, tm), lambda i: (i, 0, 0))]
    else:
        out_shape += [
            jax.ShapeDtypeStruct((B_HEADS, T, KV_LORA), F32),
            jax.ShapeDtypeStruct((T, ROPE_T), F32),
            jax.ShapeDtypeStruct((T, KCAT), F32),
        ]
        out_specs += [pl.BlockSpec((B_HEADS, tm, KV_LORA), lambda i: (0, i, 0)),
                      pl.BlockSpec((tm, ROPE_T), row),
                      pl.BlockSpec((tm, KCAT), row)]
    return pl.pallas_call(
        _proj_in_prompt_kernel if prompt else _proj_in_sample_kernel,
        grid=(n_tiles,), in_specs=in_specs, out_specs=out_specs, out_shape=out_shape,
        compiler_params=_params(("parallel",)),
        name="proj_in_prompt" if prompt else "proj_in_sample",
    )(x, w["g1"], *weights, cos_tab, sin_tab)


def _cumsum_rows(g):
    n = g.shape[0]
    row = lax.broadcasted_iota(jnp.int32, g.shape, 0)
    b = g
    shift = 1
    while shift < n:
        b = b + jnp.where(row >= shift, pltpu.roll(b, shift, axis=0), 0.0)
        shift *= 2
    return b


def _hgrn_chunk(q, fr, v, lb, st, chunk, sub):
    n_sub = chunk // sub
    md = _operand_dtype(chunk)
    md_sub = _operand_dtype(sub)
    one_m_lb = 1.0 - lb
    f_gate = lb + one_m_lb * jax.nn.sigmoid(fr)
    g = jnp.log(jnp.maximum(f_gate, F_FLOOR))
    k = one_m_lb * jax.nn.sigmoid(-fr)
    b = _cumsum_rows(g)

    o = _dot_nt((q * jnp.exp(b)).astype(md), st.astype(md))
    b_end = b[chunk - 1:chunk, :]
    kd = (k * jnp.exp(b_end - b)).astype(md)
    st_new = st * jnp.exp(b_end) + _dot(v.T.astype(md), kd)

    if n_sub > 1:
        vs = v.astype(md_sub)
        parts = [jnp.zeros((sub, A_DV), F32)]
        for i in range(1, n_sub):
            lo = i * sub
            r_i = b[lo - 1:lo, :]
            q_i = (q[lo:lo + sub] * jnp.exp(b[lo:lo + sub] - r_i)).astype(md_sub)
            k_i = (k[0:lo] * jnp.exp(r_i - b[0:lo])).astype(md_sub)
            a_i = _dot_nt(q_i, k_i).astype(md_sub)
            parts.append(_dot(a_i, vs[0:lo]))
        o = o + jnp.concatenate(parts, axis=0)

    q3 = q.reshape(n_sub, sub, A_DK)
    k3 = k.reshape(n_sub, sub, A_DK)
    b3 = (b * LOG2E).reshape(n_sub, sub, A_DK)
    v3 = v.astype(md).astype(F32).reshape(n_sub, sub, A_DV)
    tpos = lax.broadcasted_iota(jnp.int32, (n_sub, sub, A_DK), 1)
    od = jnp.zeros((n_sub, sub, A_DV), F32)
    for s in range(sub):
        d = jnp.where(tpos >= s, b3 - b3[:, s:s + 1, :], MASK_VALUE)
        wgt = q3 * k3[:, s:s + 1, :] * jnp.exp2(d)
        od = od + jnp.sum(wgt, axis=-1, keepdims=True) * v3[:, s:s + 1, :]
    return o + od.reshape(chunk, A_DV), st_new


def _hgrn_kernel(*refs, chunk, sub, has_s0):
    if has_s0:
        (qa_ref, fr_ref, v_ref, og_ref, lb_ref, ang_ref, s0_ref, oa_ref, sout_ref, st_ref) = refs
    else:
        (qa_ref, fr_ref, v_ref, og_ref, lb_ref, ang_ref, oa_ref, sout_ref, st_ref) = refs
        s0_ref = None
    j = pl.program_id(1)
    nb, lblk = qa_ref.shape[0], qa_ref.shape[1]
    n_chunks = lblk // chunk

    @pl.when(j == 0)
    def _():
        if has_s0:
            for n in range(nb):
                for hh in range(A_HEADS):
                    st_ref[n, hh] = s0_ref[n, hh].T
        else:
            st_ref[...] = jnp.zeros_like(st_ref)

    def seq_body(n, carry):
        def chunk_body(c, carry2):
            rows = pl.ds(pl.multiple_of(c * chunk, chunk), chunk)
            for hh in range(A_HEADS):
                cols = slice(hh * LANES, (hh + 1) * LANES)
                o, st_new = _hgrn_chunk(qa_ref[n, rows, cols], fr_ref[n, rows, cols], v_ref[n, rows, cols],
                                        lb_ref[:, cols], st_ref[n, hh], chunk, sub)
                st_ref[n, hh] = st_new
                on = _rms(o, ang_ref[:, cols])
                oa_ref[n, rows, cols] = (on * og_ref[n, rows, cols]).astype(oa_ref.dtype)
            return carry2
        return lax.fori_loop(0, n_chunks, chunk_body, carry, unroll=min(4, n_chunks))
    lax.fori_loop(0, nb, seq_body, 0)

    @pl.when(j == pl.num_programs(1) - 1)
    def _():
        for n in range(nb):
            for hh in range(A_HEADS):
                sout_ref[n, hh] = st_ref[n, hh].T


def _hgrn(qa, fr, v, og, lb, ang, s0, layer, nb, lblk, chunk, sub):
    B, L, _ = qa.shape
    has_s0 = s0 is not None
    grid = (B // nb, L // lblk)
    act = pl.BlockSpec((nb, lblk, A_WIDTH), lambda i, j: (i, j, 0))
    st_spec = pl.BlockSpec((nb, A_HEADS, A_DK, A_DV), lambda i, j: (i, 0, 0, 0))
    in_specs = [act, act, act, act, _full_spec((1, A_WIDTH)), _full_spec((1, A_WIDTH))]
    args = [qa, fr, v, og, lb, ang]
    if has_s0:
        in_specs.append(pl.BlockSpec((None, nb, A_HEADS, A_DK, A_DV), lambda i, j: (layer, i, 0, 0, 0)))
        args.append(s0)
    return pl.pallas_call(
        functools.partial(_hgrn_kernel, chunk=chunk, sub=sub, has_s0=has_s0),
        grid=grid, in_specs=in_specs,
        out_specs=[act, st_spec],
        out_shape=[jax.ShapeDtypeStruct((B, L, A_WIDTH), BF16),
                   jax.ShapeDtypeStruct((B, A_HEADS, A_DK, A_DV), F32)],
        scratch_shapes=[pltpu.VMEM((nb, A_HEADS, A_DV, A_DK), F32)],
        compiler_params=_params(("parallel", "arbitrary")),
        name="hgrn_sample" if has_s0 else "hgrn_prompt",
    )(*args)


def _attn_prompt_kernel(qt_ref, kh_ref, vt_ref, out_ref, m_s, l_s, acc_s, *, tq):
    qi = pl.program_id(1)
    m_s[...] = jnp.full_like(m_s, -jnp.inf)
    l_s[...] = jnp.zeros_like(l_s)
    acc_s[...] = jnp.zeros_like(acc_s)
    heads = range(B_HEADS)

    def tile(kt, diagonal):
        st = jnp.concatenate([_dot(kh_ref[kt, hh], qt_ref[hh]) for hh in heads], axis=1)
        if diagonal:
            kpos = lax.broadcasted_iota(jnp.int32, st.shape, 0)
            qpos = lax.broadcasted_iota(jnp.int32, st.shape, 1) % tq
            st = jnp.where(kpos <= qpos, st, MASK_VALUE)
        m_prev = m_s[...]
        m_new = jnp.maximum(m_prev, jnp.max(st, axis=0, keepdims=True))
        alpha = jnp.exp(m_prev - m_new)
        p = jnp.exp(st - m_new)
        l_s[...] = alpha * l_s[...] + jnp.sum(p, axis=0, keepdims=True)
        m_s[...] = m_new
        pb = p.astype(BF16)
        vt = vt_ref[kt]
        for hh in heads:
            cols = slice(hh * tq, (hh + 1) * tq)
            rows = slice(hh * B_DV, (hh + 1) * B_DV)
            acc_s[rows, :] = alpha[:, cols] * acc_s[rows, :] + _dot(vt[rows, :], pb[:, cols])

    def body(kt, carry):
        tile(kt, False)
        return carry
    lax.fori_loop(0, qi, body, 0)
    tile(qi, True)

    o_t = jnp.concatenate([acc_s[hh * B_DV:(hh + 1) * B_DV, :] / l_s[:, hh * tq:(hh + 1) * tq] for hh in heads],
                          axis=0)
    out_ref[...] = o_t.T.astype(out_ref.dtype)


def _attn_prompt(qt, kh, vt, B, L, tq):
    nq = L // tq
    return pl.pallas_call(
        functools.partial(_attn_prompt_kernel, tq=tq),
        grid=(B, nq),
        in_specs=[
            pl.BlockSpec((None, B_HEADS, LANES, tq), lambda b, i: (b * nq + i, 0, 0, 0)),
            pl.BlockSpec((nq, B_HEADS, tq, LANES), lambda b, i: (b, 0, 0, 0)),
            pl.BlockSpec((nq, B_WIDTH, tq), lambda b, i: (b, 0, 0)),
        ],
        out_specs=pl.BlockSpec((tq, B_WIDTH), lambda b, i: (b * nq + i, 0)),
        out_shape=jax.ShapeDtypeStruct((B * L, B_WIDTH), BF16),
        scratch_shapes=[pltpu.VMEM((1, B_HEADS * tq), F32), pltpu.VMEM((1, B_HEADS * tq), F32),
                        pltpu.VMEM((B_WIDTH, tq), F32)],
        compiler_params=_params(("parallel", "arbitrary")),
        name="attn_prompt",
    )(qt, kh, vt)


def _lane_tile(x, width):
    if width % LANES == 0:
        return jnp.concatenate([x] * (width // LANES), axis=1) if width > LANES else x
    return x[:, 0:width]


def _attn_sample_kernel(pt_ref, qlat_ref, qpe_ref, knew_ref, wuv_ref, ckv_hbm, krt_hbm, out_ref,
                        q_s, qp_s, k_s, kt_s, kpt_s, m_s, l_s, acc_s, kbuf, krbuf, sem,
                        *, layer, n_pages, n_seq_blk, dec):
    blk = pl.program_id(0)
    g = pl.program_id(1)
    n_groups = pl.num_programs(1)
    step = blk * n_groups + g
    n_steps = pl.num_programs(0) * n_groups
    slot = lax.rem(step, 2)
    rows = B_HEADS * dec

    def page_copies(slot_, page_of):
        out = []
        for sq in range(n_seq_blk):
            for i in range(n_pages):
                page = page_of(sq, i)
                keys = pl.ds(i * PAGE_SIZE, PAGE_SIZE)
                out.append(pltpu.make_async_copy(ckv_hbm.at[layer, page], kbuf.at[slot_, sq, keys, :], sem.at[slot_]))
                out.append(pltpu.make_async_copy(krt_hbm.at[layer, page], krbuf.at[slot_, sq, :, keys], sem.at[slot_]))
        return out

    def start_step(step_, slot_):
        blk_ = step_ // n_groups
        g_ = step_ - blk_ * n_groups
        for c in page_copies(slot_, lambda sq, i: pt_ref[blk_ * n_seq_blk + sq, g_ * n_pages + i]):
            c.start()

    @pl.when(step == 0)
    def _():
        start_step(step, slot)

    for c in page_copies(slot, lambda sq, i: 0):
        c.wait()

    @pl.when(step + 1 < n_steps)
    def _():
        start_step(step + 1, 1 - slot)

    @pl.when(g == 0)
    def _():
        for sq in range(n_seq_blk):
            q_s[sq] = qlat_ref[:, sq].reshape(rows, KV_LORA).astype(BF16)
            qpe = qpe_ref[sq]
            qp_s[sq] = jnp.concatenate(
                [qpe[:, hh * B_ROPE:(hh + 1) * B_ROPE] for hh in range(B_HEADS)], axis=0).astype(BF16)
        m_s[...] = jnp.full_like(m_s, -jnp.inf)
        l_s[...] = jnp.zeros_like(l_s)
        acc_s[...] = jnp.zeros_like(acc_s)

    def update(sq, s, vals):
        n = s.shape[1]
        m_prev = m_s[sq]
        m_new = jnp.maximum(m_prev, jnp.max(s, axis=-1, keepdims=True))
        alpha = jnp.exp(m_prev - m_new)
        p = jnp.exp(s - _lane_tile(m_new, n))
        l_s[sq] = alpha * l_s[sq] + jnp.sum(p, axis=-1, keepdims=True)
        acc_s[sq] = _lane_tile(alpha, KV_LORA) * acc_s[sq] + _dot(p.astype(vals.dtype), vals)
        m_s[sq] = m_new

    for sq in range(n_seq_blk):
        for i in range(n_pages):
            page = kbuf[slot, sq, i * PAGE_SIZE:(i + 1) * PAGE_SIZE, :]
            k_s[sq, i * PAGE_SIZE:(i + 1) * PAGE_SIZE, :] = page.astype(BF16)
            kt_s[sq, :, i * PAGE_SIZE:(i + 1) * PAGE_SIZE] = page.T.astype(BF16)
        kpt_s[sq] = krbuf[slot, sq].astype(BF16)
        kc = k_s[sq]
        update(sq, _dot(q_s[sq], kt_s[sq]) + _dot(qp_s[sq], kpt_s[sq]), kc)

    @pl.when(g == pl.num_programs(1) - 1)
    def _():
        qpos = lax.broadcasted_iota(jnp.int32, (rows, dec), 0) % dec
        kpos = lax.broadcasted_iota(jnp.int32, (rows, dec), 1)
        for sq in range(n_seq_blk):
            knew = knew_ref[sq]
            kc_new = knew[:, 0:KV_LORA]
            kp_new = knew[:, KV_LORA:KV_LORA + B_ROPE]
            s = _dot_nt(q_s[sq].astype(F32), kc_new) + _dot_nt(qp_s[sq].astype(F32), kp_new)
            update(sq, jnp.where(kpos <= qpos, s, MASK_VALUE), kc_new)
            o = (acc_s[sq] / _lane_tile(l_s[sq], KV_LORA)).astype(_operand_dtype(dec))
            for p in range(B_HEADS // 2):
                pair = jnp.concatenate([o[(2 * p) * dec:(2 * p + 1) * dec],
                                        o[(2 * p + 1) * dec:(2 * p + 2) * dec]], axis=-1)
                out_ref[sq, :, p * LANES:(p + 1) * LANES] = _dot(pair, wuv_ref[p].astype(pair.dtype))


def _attn_sample(page_table, qlat, qpe, kcat, cache_ckv, cache_krope_t, wuv, layer, n_pages, n_seq_blk):
    n_seq, n_logical = page_table.shape
    dec = qlat.shape[1] // n_seq
    rows = B_HEADS * dec
    groups = n_logical // n_pages

    seq3 = lambda n, g, pt: (n, 0, 0)
    in_specs = [
        pl.BlockSpec((B_HEADS, n_seq_blk, dec, KV_LORA), lambda n, g, pt: (0, n, 0, 0)),
        pl.BlockSpec((n_seq_blk, dec, ROPE_T), seq3),
        pl.BlockSpec((n_seq_blk, dec, KCAT), seq3),
        pl.BlockSpec(wuv.shape, lambda n, g, pt: (0, 0, 0)),
        pl.BlockSpec(memory_space=pl.ANY),
        pl.BlockSpec(memory_space=pl.ANY),
    ]
    keys = n_pages * PAGE_SIZE
    grid_spec = pltpu.PrefetchScalarGridSpec(
        num_scalar_prefetch=1, grid=(n_seq // n_seq_blk, groups), in_specs=in_specs,
        out_specs=pl.BlockSpec((n_seq_blk, dec, B_WIDTH), seq3),
        scratch_shapes=[pltpu.VMEM((n_seq_blk, rows, KV_LORA), BF16), pltpu.VMEM((n_seq_blk, rows, B_ROPE), BF16),
                        pltpu.VMEM((n_seq_blk, keys, KV_LORA), BF16),
                        pltpu.VMEM((n_seq_blk, KV_LORA, keys), BF16),
                        pltpu.VMEM((n_seq_blk, B_ROPE, keys), BF16),
                        pltpu.VMEM((n_seq_blk, rows, LANES), F32), pltpu.VMEM((n_seq_blk, rows, LANES), F32),
                        pltpu.VMEM((n_seq_blk, rows, KV_LORA), F32),
                        pltpu.VMEM((2, n_seq_blk, keys, KV_LORA), F32),
                        pltpu.VMEM((2, n_seq_blk, B_ROPE, keys), F32),
                        pltpu.SemaphoreType.DMA((2,))])
    out = pl.pallas_call(
        functools.partial(_attn_sample_kernel, layer=layer, n_pages=n_pages, n_seq_blk=n_seq_blk, dec=dec),
        grid_spec=grid_spec,
        out_shape=jax.ShapeDtypeStruct((n_seq, dec, B_WIDTH), F32),
        compiler_params=_params(("arbitrary", "arbitrary")),
        name="attn_sample",
    )(page_table, qlat.reshape(B_HEADS, n_seq, dec, KV_LORA), qpe.reshape(n_seq, dec, ROPE_T),
      kcat.reshape(n_seq, dec, KCAT), wuv, cache_ckv, cache_krope_t)
    return out.reshape(n_seq * dec, B_WIDTH)


def _post_kernel(x_ref, oa_ref, ob_ref, ga_ref, gb_ref, wbra_ref, wbrb_ref, wout_ref, g2_ref,
                 wup_ref, wdn_ref, gf_ref, out_ref, *, final):
    ma = _dot(oa_ref[...].astype(BF16), wbra_ref[...])
    mb = _dot(ob_ref[...].astype(BF16), wbrb_ref[...])
    merged = ga_ref[...].astype(F32) * ma + gb_ref[...].astype(F32) * mb
    x1 = x_ref[...] + _dot(merged.astype(BF16), wout_ref[...])
    h2 = _rms(x1, g2_ref[...]).astype(BF16)
    acc = x1
    ff_chunk = D_MODEL
    for c in range(D_FF // ff_chunk):
        u = jnp.maximum(_dot(h2, wup_ref[:, c * ff_chunk:(c + 1) * ff_chunk]), 0.0)
        acc = acc + _dot((u * u).astype(BF16), wdn_ref[c * ff_chunk:(c + 1) * ff_chunk, :])
    if final:
        acc = _rms(acc, gf_ref[...])
    out_ref[...] = acc


def _post(x, oa, ob, ga, gb, w, gf, tm, final):
    T = x.shape[0]
    row = lambda i: (i, 0)
    in_specs = [
        pl.BlockSpec((tm, D_MODEL), row), pl.BlockSpec((tm, A_WIDTH), row), pl.BlockSpec((tm, B_WIDTH), row),
        pl.BlockSpec((tm, D_MODEL), row), pl.BlockSpec((tm, D_MODEL), row),
        _full_spec(w["wbra"].shape), _full_spec(w["wbrb"].shape), _full_spec(w["wout"].shape),
        _full_spec((1, D_MODEL)), _full_spec(w["wup"].shape), _full_spec(w["wdn"].shape),
        _full_spec((1, D_MODEL)),
    ]
    return pl.pallas_call(
        functools.partial(_post_kernel, final=final),
        grid=(T // tm,), in_specs=in_specs,
        out_specs=pl.BlockSpec((tm, D_MODEL), row),
        out_shape=jax.ShapeDtypeStruct((T, D_MODEL), F32),
        compiler_params=_params(("parallel",)),
        name="post_final" if final else "post",
    )(x, oa, ob, ga, gb, w["wbra"], w["wbrb"], w["wout"], w["g2"], w["wup"], w["wdn"], gf)


def _rot_half(w):
    half = w.shape[-1] // 2
    return jnp.concatenate([-w[..., half:], w[..., :half]], axis=-1)


def _layer_weights(l, lbs, norm1_g, w_in, a_norm_g, q_norm_g, w_uq, kv_norm_g, w_uk, w_uv,
                   w_br_a, w_br_b, w_out, norm2_g, w_up, w_down):
    wi = w_in[l]
    c_b = 4 * A_WIDTH
    c_rope = c_b + Q_LORA + KV_LORA
    c_g = c_rope + B_ROPE
    rope = wi[:, c_rope:c_g]
    wb = jnp.concatenate([wi[:, c_b:c_rope], jnp.tile(rope, (1, B_HEADS)),
                          jnp.tile(_rot_half(rope), (1, B_HEADS))], axis=-1)
    uq = w_uq[l].reshape(Q_LORA, B_HEADS, B_NOPE + B_ROPE)
    nope = uq[:, :, :B_NOPE]
    wqn = jnp.pad(nope, ((0, 0), (0, 0), (0, LANES - B_NOPE))).reshape(Q_LORA, B_HEADS * LANES)
    pe = uq[:, :, B_NOPE:]
    tail = LANES - B_ROPE - B_NOPE
    wqf = jnp.pad(jnp.concatenate([pe, nope], axis=-1), ((0, 0), (0, 0), (0, tail))).reshape(Q_LORA, B_HEADS * LANES)
    wqr = jnp.pad(_rot_half(pe), ((0, 0), (0, 0), (0, LANES - B_ROPE))).reshape(Q_LORA, B_HEADS * LANES)
    lane_pad = ((0, 0), (0, LANES - B_ROPE))
    wbp = jnp.concatenate([wi[:, c_b:c_rope], jnp.pad(rope, lane_pad), jnp.pad(_rot_half(rope), lane_pad)], axis=-1)
    wukf = jnp.pad(w_uk[l], ((0, 0), (0, 0), (B_ROPE, tail)))
    wukf = jnp.swapaxes(wukf, 0, 1).reshape(KV_LORA, B_HEADS * LANES)
    wuvf = jnp.swapaxes(w_uv[l], 1, 2).reshape(B_WIDTH, KV_LORA)
    wqp = jnp.concatenate([pe.reshape(Q_LORA, ROPE_T), _rot_half(pe).reshape(Q_LORA, ROPE_T)], axis=-1)
    wuk = jnp.pad(jnp.swapaxes(w_uk[l], 1, 2), ((0, 0), (0, LANES - B_NOPE), (0, 0)))
    uv = w_uv[l]
    zeros = jnp.zeros_like(uv[0])
    wuv = jnp.stack([jnp.concatenate([jnp.concatenate([uv[2 * p], zeros], axis=-1),
                                      jnp.concatenate([zeros, uv[2 * p + 1]], axis=-1)], axis=0)
                     for p in range(B_HEADS // 2)])
    bf = lambda a: a.astype(BF16)
    return dict(
        g1=norm1_g[l].reshape(1, D_MODEL), wa=bf(wi[:, :c_b]), wb=bf(wb), wg=bf(wi[:, c_g:]),
        qn=q_norm_g[l].reshape(1, Q_LORA), kvn=kv_norm_g[l].reshape(1, KV_LORA),
        wqn=bf(wqn), wqp=bf(wqp), wuk=bf(wuk), wuv=bf(wuv),
        wbp=bf(wbp), wqf=bf(wqf), wqr=bf(wqr), wukf=bf(wukf), wuvf=bf(wuvf),
        lb=lbs[l].reshape(1, A_WIDTH), ang=a_norm_g[l].reshape(1, A_WIDTH),
        wbra=bf(w_br_a[l]), wbrb=bf(w_br_b[l]), wout=bf(w_out[l]),
        g2=norm2_g[l].reshape(1, D_MODEL), wup=bf(w_up[l]), wdn=bf(w_down[l]),
    )


def _rope_angles(pos):
    half = B_ROPE // 2
    inv = ROPE_THETA ** (-jnp.arange(half, dtype=F32) / half)
    ang = pos.astype(F32)[:, None] * inv[None, :]
    return jnp.tile(jnp.cos(ang), (1, 2)), jnp.tile(jnp.sin(ang), (1, 2))


def _rope_tables_tiled(pos):
    cos, sin = _rope_angles(pos)
    return jnp.tile(cos, (1, B_HEADS)), jnp.tile(sin, (1, B_HEADS))


def _rope_tables_block(pos):
    cos, sin = _rope_angles(pos)
    pad = ((0, 0), (0, LANES - B_ROPE))
    return jnp.pad(cos, pad, constant_values=1.0), jnp.pad(sin, pad)


def kernel(x_prompt, x_sample, cache_ckv, cache_krope, state_hgrn, page_table, norm1_g, w_in, lower_bounds,
           a_norm_g, q_norm_g, w_uq, kv_norm_g, w_uk, w_uv, w_br_a, w_br_b, w_out, norm2_g, w_up, w_down,
           final_norm_g):
    B, L, _ = x_prompt.shape
    n_seq, dec, _ = x_sample.shape
    n_logical = page_table.shape[1]
    past_len = n_logical * PAGE_SIZE
    tq = 256
    tm_s = 256
    pages_per_step = min(32, n_logical)
    seqs_per_step = 2

    lbs = _lower_bounds(lower_bounds)
    cos_p, sin_p = _rope_tables_block(jnp.arange(L, dtype=jnp.int32))
    pos_s = past_len + (jnp.arange(tm_s, dtype=jnp.int32) % dec)
    cos_s, sin_s = _rope_tables_tiled(pos_s)
    gf = final_norm_g.reshape(1, D_MODEL)
    cache_krope_t = jnp.swapaxes(cache_krope, 2, 3)

    xp = x_prompt.reshape(B * L, D_MODEL)
    xs = x_sample.reshape(n_seq * dec, D_MODEL)
    outs = [[] for _ in range(6)]
    for l in range(DEPTH):
        w = _layer_weights(l, lbs, norm1_g, w_in, a_norm_g, q_norm_g, w_uq, kv_norm_g, w_uk, w_uv,
                           w_br_a, w_br_b, w_out, norm2_g, w_up, w_down)
        final = l == DEPTH - 1

        qa, fr, v, og, ga, gb, ckv, kpe, qt, kh, vt = _proj_in(xp, w, cos_p, sin_p, tq, True)
        to_seq = lambda a: a.reshape(B, L, A_WIDTH)
        oa, s_p = _hgrn(to_seq(qa), to_seq(fr), to_seq(v), to_seq(og), w["lb"], w["ang"], None, l,
                        nb=1, lblk=256, chunk=64, sub=16)
        ob = _attn_prompt(qt, kh, vt, B, L, tq)
        xp = _post(xp, oa.reshape(B * L, A_WIDTH), ob, ga, gb, w, gf, tq, final)
        outs[0].append(ckv.reshape(B, L, KV_LORA))
        outs[1].append(kpe.reshape(B, L, B_ROPE))
        outs[2].append(s_p)

        qa, fr, v, og, ga, gb, ckv, kpe, qlat, qpe, kcat = _proj_in(xs, w, cos_s, sin_s, tm_s, False)
        to_seq = lambda a: a.reshape(n_seq, dec, A_WIDTH)
        oa, s_s = _hgrn(to_seq(qa), to_seq(fr), to_seq(v), to_seq(og), w["lb"], w["ang"], state_hgrn, l,
                        nb=8, lblk=dec, chunk=dec, sub=dec)
        ob = _attn_sample(page_table, qlat, qpe, kcat, cache_ckv, cache_krope_t, w["wuv"], l, pages_per_step,
                          seqs_per_step)
        xs = _post(xs, oa.reshape(n_seq * dec, A_WIDTH), ob, ga, gb, w, gf, tm_s, final)
        outs[3].append(ckv.reshape(n_seq, dec, KV_LORA))
        outs[4].append(kpe.reshape(n_seq, dec, B_ROPE))
        outs[5].append(s_s)

    return (xp.reshape(B, L, D_MODEL), xs.reshape(n_seq, dec, D_MODEL),
            jnp.stack(outs[0]), jnp.stack(outs[1]), jnp.stack(outs[2]),
            jnp.stack(outs[3]), jnp.stack(outs[4]), jnp.stack(outs[5]))
```

```python
import functools

import jax
import jax.numpy as jnp
from jax import lax
from jax.experimental import pallas as pl
from jax.experimental.pallas import tpu as pltpu

D_MODEL = 1024
DEPTH = 4
PAGE_SIZE = 128
A_HEADS = 4
A_DK = 128
A_DV = 128
A_WIDTH = A_HEADS * A_DV
B_HEADS = 8
B_NOPE = 64
B_ROPE = 32
B_DV = 64
B_WIDTH = B_HEADS * B_DV
Q_LORA = 256
KV_LORA = 256
ROPE_THETA = 10000.0
D_FF = 4 * D_MODEL
EPS = 1e-6
MASK_VALUE = -1e30
F_FLOOR = 1e-30
SM_SCALE = (B_NOPE + B_ROPE) ** -0.5
LOG2E = 1.4426950408889634

LANES = 128
PACKED_ROWS = 16
ROPE_T = B_HEADS * B_ROPE
KCAT = KV_LORA + ROPE_T
VMEM_LIMIT = 56 * 1024 * 1024

BF16 = jnp.bfloat16
F32 = jnp.float32

_NT = (((1,), (1,)), ((), ()))


def _dot(a, b):
    return jnp.dot(a, b, preferred_element_type=F32)


def _dot_nt(a, b):
    return lax.dot_general(a, b, _NT, preferred_element_type=F32)


def _full_spec(shape):
    nd = len(shape)
    return pl.BlockSpec(shape, lambda *_: (0,) * nd)


def _params(sem):
    return pltpu.CompilerParams(dimension_semantics=sem, vmem_limit_bytes=VMEM_LIMIT)


def _rms(x, g):
    ms = jnp.mean(x * x, axis=-1, keepdims=True)
    return x * lax.rsqrt(ms + EPS) * g


def _operand_dtype(rows):
    return BF16 if rows >= PACKED_ROWS else F32


def _lbs_kernel(lb_ref, out_ref):
    x = lb_ref[...]
    m = jnp.max(x, axis=0, keepdims=True)
    e = jnp.exp(x - m)
    p = e / jnp.sum(e, axis=0, keepdims=True)
    acc = jnp.zeros_like(p[0:1])
    out_ref[0:1, :] = acc
    for l in range(1, DEPTH):
        acc = acc + p[l:l + 1]
        out_ref[l:l + 1, :] = acc


def _lower_bounds(lower_bounds):
    return pl.pallas_call(
        _lbs_kernel,
        out_shape=jax.ShapeDtypeStruct(lower_bounds.shape, F32),
        name="hgrn_lower_bounds",
    )(lower_bounds.astype(F32))


def _mixer_a_and_gates(h, wa_ref, wg_ref, qa_ref, fr_ref, v_ref, og_ref, ga_ref, gb_ref):
    z = _dot(h, wa_ref[:, 0:A_WIDTH])
    qa_ref[...] = z * jax.nn.sigmoid(z)
    fr_ref[...] = _dot(h, wa_ref[:, A_WIDTH:2 * A_WIDTH])
    v_ref[...] = _dot(h, wa_ref[:, 2 * A_WIDTH:3 * A_WIDTH])
    z = _dot(h, wa_ref[:, 3 * A_WIDTH:4 * A_WIDTH])
    og_ref[...] = z * jax.nn.sigmoid(z)
    half = D_MODEL // 2
    for c in range(2):
        ga_ref[:, c * half:(c + 1) * half] = jax.nn.sigmoid(
            _dot(h, wg_ref[:, c * half:(c + 1) * half])).astype(ga_ref.dtype)
        gb_ref[:, c * half:(c + 1) * half] = jax.nn.sigmoid(
            _dot(h, wg_ref[:, D_MODEL + c * half:D_MODEL + (c + 1) * half])).astype(gb_ref.dtype)


def _proj_in_sample_kernel(x_ref, g1_ref, wa_ref, wb_ref, wg_ref, qn_ref, kvn_ref, wqn_ref, wqp_ref,
                           wuk_ref, cos_ref, sin_ref,
                           qa_ref, fr_ref, v_ref, og_ref, ga_ref, gb_ref, ckv_ref, kpe_ref,
                           qlat_ref, qpe_ref, kcat_ref):
    h = _rms(x_ref[...], g1_ref[...]).astype(BF16)
    _mixer_a_and_gates(h, wa_ref, wg_ref, qa_ref, fr_ref, v_ref, og_ref, ga_ref, gb_ref)

    zb = _dot(h, wb_ref[...])
    cos = cos_ref[...]
    sin = sin_ref[...]
    cq = _rms(zb[:, 0:Q_LORA], qn_ref[...]).astype(BF16)
    ckv = _rms(zb[:, Q_LORA:Q_LORA + KV_LORA], kvn_ref[...])
    off = Q_LORA + KV_LORA
    kpe = zb[:, off:off + ROPE_T] * cos + zb[:, off + ROPE_T:off + 2 * ROPE_T] * sin
    ckv_ref[...] = ckv
    kpe_ref[...] = kpe[:, 0:B_ROPE]
    kcat_ref[:, 0:KV_LORA] = ckv
    kcat_ref[:, KV_LORA:KCAT] = kpe

    qp2 = _dot(cq, wqp_ref[...])
    qpe_ref[...] = (qp2[:, 0:ROPE_T] * cos + qp2[:, ROPE_T:2 * ROPE_T] * sin) * SM_SCALE
    qn = _dot(cq, wqn_ref[...]).astype(BF16)
    for hh in range(B_HEADS):
        qlat_ref[hh] = _dot(qn[:, hh * LANES:(hh + 1) * LANES], wuk_ref[hh]) * SM_SCALE


def _proj_in_prompt_kernel(x_ref, g1_ref, wa_ref, wb_ref, wg_ref, qn_ref, kvn_ref, wqf_ref, wqr_ref,
                           wukf_ref, wuvf_ref, cos_ref, sin_ref,
                           qa_ref, fr_ref, v_ref, og_ref, ga_ref, gb_ref, ckv_ref, kpe_ref,
                           qt_ref, kh_ref, vt_ref):
    h = _rms(x_ref[...], g1_ref[...]).astype(BF16)
    _mixer_a_and_gates(h, wa_ref, wg_ref, qa_ref, fr_ref, v_ref, og_ref, ga_ref, gb_ref)

    zb = _dot(h, wb_ref[...])
    cos = cos_ref[...]
    sin = sin_ref[...]
    cq = _rms(zb[:, 0:Q_LORA], qn_ref[...]).astype(BF16)
    ckv = _rms(zb[:, Q_LORA:Q_LORA + KV_LORA], kvn_ref[...])
    off = Q_LORA + KV_LORA
    kpe = zb[:, off:off + LANES] * cos + zb[:, off + LANES:off + 2 * LANES] * sin
    ckv_ref[...] = ckv
    kpe_ref[...] = kpe[:, 0:B_ROPE]

    kn = _dot(ckv.astype(BF16), wukf_ref[...])
    vt_ref[...] = _dot(wuvf_ref[...], ckv.T.astype(BF16)).astype(BF16)
    zq = _dot(cq, wqf_ref[...])
    zr = _dot(cq, wqr_ref[...])
    for hh in range(B_HEADS):
        blk = slice(hh * LANES, (hh + 1) * LANES)
        kh_ref[hh] = (kn[:, blk] + kpe).astype(BF16)
        qf = (zq[:, blk] * cos + zr[:, blk] * sin) * SM_SCALE
        qt_ref[hh] = qf.T.astype(BF16)


def _proj_in(x, w, cos_tab, sin_tab, tm, prompt):
    T = x.shape[0]
    n_tab = cos_tab.shape[0] // tm
    tab_w = cos_tab.shape[1]
    n_tiles = T // tm
    row = lambda i: (i, 0)
    tab = lambda i: (i % n_tab, 0)
    if prompt:
        weights = [w["wa"], w["wbp"], w["wg"], w["qn"], w["kvn"], w["wqf"], w["wqr"], w["wukf"], w["wuvf"]]
    else:
        weights = [w["wa"], w["wb"], w["wg"], w["qn"], w["kvn"], w["wqn"], w["wqp"], w["wuk"]]
    in_specs = ([pl.BlockSpec((tm, D_MODEL), row), _full_spec((1, D_MODEL))]
                + [_full_spec(a.shape) for a in weights]
                + [pl.BlockSpec((tm, tab_w), tab), pl.BlockSpec((tm, tab_w), tab)])
    out_shape = [
        jax.ShapeDtypeStruct((T, A_WIDTH), F32),
        jax.ShapeDtypeStruct((T, A_WIDTH), F32),
        jax.ShapeDtypeStruct((T, A_WIDTH), F32),
        jax.ShapeDtypeStruct((T, A_WIDTH), F32),
        jax.ShapeDtypeStruct((T, D_MODEL), BF16),
        jax.ShapeDtypeStruct((T, D_MODEL), BF16),
        jax.ShapeDtypeStruct((T, KV_LORA), F32),
        jax.ShapeDtypeStruct((T, B_ROPE), F32),
    ]
    out_specs = [
        pl.BlockSpec((tm, A_WIDTH), row), pl.BlockSpec((tm, A_WIDTH), row),
        pl.BlockSpec((tm, A_WIDTH), row), pl.BlockSpec((tm, A_WIDTH), row),
        pl.BlockSpec((tm, D_MODEL), row), pl.BlockSpec((tm, D_MODEL), row),
        pl.BlockSpec((tm, KV_LORA), row),
        pl.BlockSpec((tm, B_ROPE), row),
    ]
    if prompt:
        tile4 = lambda i: (i, 0, 0, 0)
        out_shape += [
            jax.ShapeDtypeStruct((n_tiles, B_HEADS, LANES, tm), BF16),
            jax.ShapeDtypeStruct((n_tiles, B_HEADS, tm, LANES), BF16),
            jax.ShapeDtypeStruct((n_tiles, B_WIDTH, tm), BF16),
        ]
        out_specs += [pl.BlockSpec((None, B_HEADS, LANES, tm), tile4),
                      pl.BlockSpec((None, B_HEADS, tm, LANES), tile4),
                      pl.BlockSpec((None, B_WIDTH, tm), lambda i: (i, 0, 0))]
    else:
        out_shape += [
            jax.ShapeDtypeStruct((B_HEADS, T, KV_LORA), F32),
            jax.ShapeDtypeStruct((T, ROPE_T), F32),
            jax.ShapeDtypeStruct((T, KCAT), F32),
        ]
        out_specs += [pl.BlockSpec((B_HEADS, tm, KV_LORA), lambda i: (0, i, 0)),
                      pl.BlockSpec((tm, ROPE_T), row),
                      pl.BlockSpec((tm, KCAT), row)]
    return pl.pallas_call(
        _proj_in_prompt_kernel if prompt else _proj_in_sample_kernel,
        grid=(n_tiles,), in_specs=in_specs, out_specs=out_specs, out_shape=out_shape,
        compiler_params=_params(("parallel",)),
        name="proj_in_prompt" if prompt else "proj_in_sample",
    )(x, w["g1"], *weights, cos_tab, sin_tab)


def _cumsum_rows(g):
    n = g.shape[0]
    row = lax.broadcasted_iota(jnp.int32, g.shape, 0)
    b = g
    shift = 1
    while shift < n:
        b = b + jnp.where(row >= shift, pltpu.roll(b, shift, axis=0), 0.0)
        shift *= 2
    return b


def _hgrn_chunk(q, fr, v, lb, st, chunk, sub):
    n_sub = chunk // sub
    md = _operand_dtype(chunk)
    md_sub = _operand_dtype(sub)
    one_m_lb = 1.0 - lb
    f_gate = lb + one_m_lb * jax.nn.sigmoid(fr)
    g = jnp.log(jnp.maximum(f_gate, F_FLOOR))
    k = one_m_lb * jax.nn.sigmoid(-fr)
    b = _cumsum_rows(g)

    o = _dot_nt((q * jnp.exp(b)).astype(md), st.astype(md))
    b_end = b[chunk - 1:chunk, :]
    kd = (k * jnp.exp(b_end - b)).astype(md)
    st_new = st * jnp.exp(b_end) + _dot(v.T.astype(md), kd)

    if n_sub > 1:
        vs = v.astype(md_sub)
        parts = [jnp.zeros((sub, A_DV), F32)]
        for i in range(1, n_sub):
            lo = i * sub
            r_i = b[lo - 1:lo, :]
            q_i = (q[lo:lo + sub] * jnp.exp(b[lo:lo + sub] - r_i)).astype(md_sub)
            k_i = (k[0:lo] * jnp.exp(r_i - b[0:lo])).astype(md_sub)
            a_i = _dot_nt(q_i, k_i).astype(md_sub)
            parts.append(_dot(a_i, vs[0:lo]))
        o = o + jnp.concatenate(parts, axis=0)

    q3 = q.reshape(n_sub, sub, A_DK)
    k3 = k.reshape(n_sub, sub, A_DK)
    b3 = (b * LOG2E).reshape(n_sub, sub, A_DK)
    v3 = v.astype(md).astype(F32).reshape(n_sub, sub, A_DV)
    tpos = lax.broadcasted_iota(jnp.int32, (n_sub, sub, A_DK), 1)
    od = jnp.zeros((n_sub, sub, A_DV), F32)
    for s in range(sub):
        d = jnp.where(tpos >= s, b3 - b3[:, s:s + 1, :], MASK_VALUE)
        wgt = q3 * k3[:, s:s + 1, :] * jnp.exp2(d)
        od = od + jnp.sum(wgt, axis=-1, keepdims=True) * v3[:, s:s + 1, :]
    return o + od.reshape(chunk, A_DV), st_new


def _hgrn_kernel(*refs, chunk, sub, has_s0):
    if has_s0:
        (qa_ref, fr_ref, v_ref, og_ref, lb_ref, ang_ref, s0_ref, oa_ref, sout_ref, st_ref) = refs
    else:
        (qa_ref, fr_ref, v_ref, og_ref, lb_ref, ang_ref, oa_ref, sout_ref, st_ref) = refs
        s0_ref = None
    j = pl.program_id(1)
    nb, lblk = qa_ref.shape[0], qa_ref.shape[1]
    n_chunks = lblk // chunk

    @pl.when(j == 0)
    def _():
        if has_s0:
            for n in range(nb):
                for hh in range(A_HEADS):
                    st_ref[n, hh] = s0_ref[n, hh].T
        else:
            st_ref[...] = jnp.zeros_like(st_ref)

    def seq_body(n, carry):
        def chunk_body(c, carry2):
            rows = pl.ds(pl.multiple_of(c * chunk, chunk), chunk)
            for hh in range(A_HEADS):
                cols = slice(hh * LANES, (hh + 1) * LANES)
                o, st_new = _hgrn_chunk(qa_ref[n, rows, cols], fr_ref[n, rows, cols], v_ref[n, rows, cols],
                                        lb_ref[:, cols], st_ref[n, hh], chunk, sub)
                st_ref[n, hh] = st_new
                on = _rms(o, ang_ref[:, cols])
                oa_ref[n, rows, cols] = (on * og_ref[n, rows, cols]).astype(oa_ref.dtype)
            return carry2
        return lax.fori_loop(0, n_chunks, chunk_body, carry, unroll=min(4, n_chunks))
    lax.fori_loop(0, nb, seq_body, 0)

    @pl.when(j == pl.num_programs(1) - 1)
    def _():
        for n in range(nb):
            for hh in range(A_HEADS):
                sout_ref[n, hh] = st_ref[n, hh].T


def _hgrn(qa, fr, v, og, lb, ang, s0, layer, nb, lblk, chunk, sub):
    B, L, _ = qa.shape
    has_s0 = s0 is not None
    grid = (B // nb, L // lblk)
    act = pl.BlockSpec((nb, lblk, A_WIDTH), lambda i, j: (i, j, 0))
    st_spec = pl.BlockSpec((nb, A_HEADS, A_DK, A_DV), lambda i, j: (i, 0, 0, 0))
    in_specs = [act, act, act, act, _full_spec((1, A_WIDTH)), _full_spec((1, A_WIDTH))]
    args = [qa, fr, v, og, lb, ang]
    if has_s0:
        in_specs.append(pl.BlockSpec((None, nb, A_HEADS, A_DK, A_DV), lambda i, j: (layer, i, 0, 0, 0)))
        args.append(s0)
    return pl.pallas_call(
        functools.partial(_hgrn_kernel, chunk=chunk, sub=sub, has_s0=has_s0),
        grid=grid, in_specs=in_specs,
        out_specs=[act, st_spec],
        out_shape=[jax.ShapeDtypeStruct((B, L, A_WIDTH), BF16),
                   jax.ShapeDtypeStruct((B, A_HEADS, A_DK, A_DV), F32)],
        scratch_shapes=[pltpu.VMEM((nb, A_HEADS, A_DV, A_DK), F32)],
        compiler_params=_params(("parallel", "arbitrary")),
        name="hgrn_sample" if has_s0 else "hgrn_prompt",
    )(*args)


def _attn_prompt_kernel(qt_ref, kh_ref, vt_ref, out_ref, m_s, l_s, acc_s, *, tm, tiles, heads_per_group):
    qi = pl.program_id(1)
    tq = tiles * tm
    gw = heads_per_group * tq
    m_s[...] = jnp.full_like(m_s, -jnp.inf)
    l_s[...] = jnp.zeros_like(l_s)
    acc_s[...] = jnp.zeros_like(acc_s)

    def tile(kt, diagonal):
        vt = jnp.concatenate([vt_ref[kt * tiles + j] for j in range(tiles)], axis=1)
        for grp in range(B_HEADS // heads_per_group):
            heads = range(grp * heads_per_group, (grp + 1) * heads_per_group)
            gcols = slice(grp * gw, (grp + 1) * gw)
            parts = []
            for hh in heads:
                keys = jnp.concatenate([kh_ref[kt * tiles + j, hh] for j in range(tiles)], axis=0)
                qry = jnp.concatenate([qt_ref[j, hh] for j in range(tiles)], axis=1)
                parts.append(_dot(keys, qry))
            st = jnp.concatenate(parts, axis=1)
            if diagonal:
                kpos = lax.broadcasted_iota(jnp.int32, st.shape, 0)
                qpos = lax.broadcasted_iota(jnp.int32, st.shape, 1) % tq
                st = jnp.where(kpos <= qpos, st, MASK_VALUE)
            m_prev = m_s[:, gcols]
            m_new = jnp.maximum(m_prev, jnp.max(st, axis=0, keepdims=True))
            alpha = jnp.exp(m_prev - m_new)
            p = jnp.exp(st - m_new)
            l_s[:, gcols] = alpha * l_s[:, gcols] + jnp.sum(p, axis=0, keepdims=True)
            m_s[:, gcols] = m_new
            pb = p.astype(BF16)
            for i, hh in enumerate(heads):
                cols = slice(i * tq, (i + 1) * tq)
                rows = slice(hh * B_DV, (hh + 1) * B_DV)
                acc_s[rows, :] = alpha[:, cols] * acc_s[rows, :] + _dot(vt[rows, :], pb[:, cols])

    def body(kt, carry):
        tile(kt, False)
        return carry
    lax.fori_loop(0, qi, body, 0)
    tile(qi, True)

    o_t = jnp.concatenate([acc_s[hh * B_DV:(hh + 1) * B_DV, :] / l_s[:, hh * tq:(hh + 1) * tq]
                           for hh in range(B_HEADS)], axis=0)
    out_ref[...] = o_t.T.astype(out_ref.dtype)


def _attn_prompt(qt, kh, vt, B, L, tm, tiles, heads_per_group):
    tq = tiles * tm
    nq = L // tq
    n_tm = L // tm
    return pl.pallas_call(
        functools.partial(_attn_prompt_kernel, tm=tm, tiles=tiles, heads_per_group=heads_per_group),
        grid=(B, nq),
        in_specs=[
            pl.BlockSpec((tiles, B_HEADS, LANES, tm), lambda b, i: (b * nq + i, 0, 0, 0)),
            pl.BlockSpec((n_tm, B_HEADS, tm, LANES), lambda b, i: (b, 0, 0, 0)),
            pl.BlockSpec((n_tm, B_WIDTH, tm), lambda b, i: (b, 0, 0)),
        ],
        out_specs=pl.BlockSpec((tq, B_WIDTH), lambda b, i: (b * nq + i, 0)),
        out_shape=jax.ShapeDtypeStruct((B * L, B_WIDTH), BF16),
        scratch_shapes=[pltpu.VMEM((1, B_HEADS * tq), F32), pltpu.VMEM((1, B_HEADS * tq), F32),
                        pltpu.VMEM((B_WIDTH, tq), F32)],
        compiler_params=_params(("parallel", "arbitrary")),
        name="attn_prompt",
    )(qt, kh, vt)


def _lane_tile(x, width):
    if width % LANES == 0:
        return jnp.concatenate([x] * (width // LANES), axis=1) if width > LANES else x
    return x[:, 0:width]


def _attn_sample_kernel(pt_ref, qlat_ref, qpe_ref, knew_ref, wuv_ref, ckv_hbm, krt_hbm, out_ref,
                        q_s, qp_s, k_s, kt_s, kpt_s, m_s, l_s, acc_s, kbuf, krbuf, sem,
                        *, layer, n_pages, n_seq_blk, dec):
    blk = pl.program_id(0)
    g = pl.program_id(1)
    n_groups = pl.num_programs(1)
    step = blk * n_groups + g
    n_steps = pl.num_programs(0) * n_groups
    slot = lax.rem(step, 2)
    rows = B_HEADS * dec

    def page_copies(slot_, page_of):
        out = []
        for sq in range(n_seq_blk):
            for i in range(n_pages):
                page = page_of(sq, i)
                keys = pl.ds(i * PAGE_SIZE, PAGE_SIZE)
                out.append(pltpu.make_async_copy(ckv_hbm.at[layer, page], kbuf.at[slot_, sq, keys, :], sem.at[slot_]))
                out.append(pltpu.make_async_copy(krt_hbm.at[layer, page], krbuf.at[slot_, sq, :, keys], sem.at[slot_]))
        return out

    def start_step(step_, slot_):
        blk_ = step_ // n_groups
        g_ = step_ - blk_ * n_groups
        for c in page_copies(slot_, lambda sq, i: pt_ref[blk_ * n_seq_blk + sq, g_ * n_pages + i]):
            c.start()

    @pl.when(step == 0)
    def _():
        start_step(step, slot)

    for c in page_copies(slot, lambda sq, i: 0):
        c.wait()

    @pl.when(step + 1 < n_steps)
    def _():
        start_step(step + 1, 1 - slot)

    @pl.when(g == 0)
    def _():
        for sq in range(n_seq_blk):
            seq = blk * n_seq_blk + sq
            q_s[sq] = qlat_ref[:, seq].reshape(rows, KV_LORA).astype(BF16)
            qpe = qpe_ref[seq]
            qp_s[sq] = jnp.concatenate(
                [qpe[:, hh * B_ROPE:(hh + 1) * B_ROPE] for hh in range(B_HEADS)], axis=0).astype(BF16)
        m_s[...] = jnp.full_like(m_s, -jnp.inf)
        l_s[...] = jnp.zeros_like(l_s)
        acc_s[...] = jnp.zeros_like(acc_s)

    def update(sq, s, vals):
        n = s.shape[1]
        m_prev = m_s[sq]
        m_new = jnp.maximum(m_prev, jnp.max(s, axis=-1, keepdims=True))
        alpha = jnp.exp(m_prev - m_new)
        p = jnp.exp(s - _lane_tile(m_new, n))
        l_s[sq] = alpha * l_s[sq] + jnp.sum(p, axis=-1, keepdims=True)
        acc_s[sq] = _lane_tile(alpha, KV_LORA) * acc_s[sq] + _dot(p.astype(vals.dtype), vals)
        m_s[sq] = m_new

    for sq in range(n_seq_blk):
        for i in range(n_pages):
            page = kbuf[slot, sq, i * PAGE_SIZE:(i + 1) * PAGE_SIZE, :]
            k_s[sq, i * PAGE_SIZE:(i + 1) * PAGE_SIZE, :] = page.astype(BF16)
            kt_s[sq, :, i * PAGE_SIZE:(i + 1) * PAGE_SIZE] = page.T.astype(BF16)
        kpt_s[sq] = krbuf[slot, sq].astype(BF16)
        kc = k_s[sq]
        update(sq, _dot(q_s[sq], kt_s[sq]) + _dot(qp_s[sq], kpt_s[sq]), kc)

    @pl.when(g == pl.num_programs(1) - 1)
    def _():
        qpos = lax.broadcasted_iota(jnp.int32, (rows, dec), 0) % dec
        kpos = lax.broadcasted_iota(jnp.int32, (rows, dec), 1)
        for sq in range(n_seq_blk):
            seq = blk * n_seq_blk + sq
            knew = knew_ref[seq]
            kc_new = knew[:, 0:KV_LORA]
            kp_new = knew[:, KV_LORA:KV_LORA + B_ROPE]
            s = _dot_nt(q_s[sq].astype(F32), kc_new) + _dot_nt(qp_s[sq].astype(F32), kp_new)
            update(sq, jnp.where(kpos <= qpos, s, MASK_VALUE), kc_new)
            o = (acc_s[sq] / _lane_tile(l_s[sq], KV_LORA)).astype(_operand_dtype(dec))
            for p in range(B_HEADS // 2):
                pair = jnp.concatenate([o[(2 * p) * dec:(2 * p + 1) * dec],
                                        o[(2 * p + 1) * dec:(2 * p + 2) * dec]], axis=-1)
                out_ref[seq, :, p * LANES:(p + 1) * LANES] = _dot(pair, wuv_ref[p].astype(pair.dtype))


def _attn_sample(page_table, qlat, qpe, kcat, cache_ckv, cache_krope_t, wuv, layer, n_pages, n_seq_blk):
    n_seq, n_logical = page_table.shape
    dec = qlat.shape[1] // n_seq
    rows = B_HEADS * dec
    groups = n_logical // n_pages

    whole3 = lambda n, g, pt: (0, 0, 0)
    in_specs = [
        pl.BlockSpec((B_HEADS, n_seq, dec, KV_LORA), lambda n, g, pt: (0, 0, 0, 0)),
        pl.BlockSpec((n_seq, dec, ROPE_T), whole3),
        pl.BlockSpec((n_seq, dec, KCAT), whole3),
        pl.BlockSpec(wuv.shape, whole3),
        pl.BlockSpec(memory_space=pl.ANY),
        pl.BlockSpec(memory_space=pl.ANY),
    ]
    keys = n_pages * PAGE_SIZE
    grid_spec = pltpu.PrefetchScalarGridSpec(
        num_scalar_prefetch=1, grid=(n_seq // n_seq_blk, groups), in_specs=in_specs,
        out_specs=pl.BlockSpec((n_seq, dec, B_WIDTH), whole3),
        scratch_shapes=[pltpu.VMEM((n_seq_blk, rows, KV_LORA), BF16), pltpu.VMEM((n_seq_blk, rows, B_ROPE), BF16),
                        pltpu.VMEM((n_seq_blk, keys, KV_LORA), BF16),
                        pltpu.VMEM((n_seq_blk, KV_LORA, keys), BF16),
                        pltpu.VMEM((n_seq_blk, B_ROPE, keys), BF16),
                        pltpu.VMEM((n_seq_blk, rows, LANES), F32), pltpu.VMEM((n_seq_blk, rows, LANES), F32),
                        pltpu.VMEM((n_seq_blk, rows, KV_LORA), F32),
                        pltpu.VMEM((2, n_seq_blk, keys, KV_LORA), F32),
                        pltpu.VMEM((2, n_seq_blk, B_ROPE, keys), F32),
                        pltpu.SemaphoreType.DMA((2,))])
    out = pl.pallas_call(
        functools.partial(_attn_sample_kernel, layer=layer, n_pages=n_pages, n_seq_blk=n_seq_blk, dec=dec),
        grid_spec=grid_spec,
        out_shape=jax.ShapeDtypeStruct((n_seq, dec, B_WIDTH), F32),
        compiler_params=_params(("arbitrary", "arbitrary")),
        name="attn_sample",
    )(page_table, qlat.reshape(B_HEADS, n_seq, dec, KV_LORA), qpe.reshape(n_seq, dec, ROPE_T),
      kcat.reshape(n_seq, dec, KCAT), wuv, cache_ckv, cache_krope_t)
    return out.reshape(n_seq * dec, B_WIDTH)


def _post_kernel(x_ref, oa_ref, ob_ref, ga_ref, gb_ref, wbra_ref, wbrb_ref, wout_ref, g2_ref,
                 wup_ref, wdn_ref, gf_ref, out_ref, *, final):
    ma = _dot(oa_ref[...].astype(BF16), wbra_ref[...])
    mb = _dot(ob_ref[...].astype(BF16), wbrb_ref[...])
    merged = ga_ref[...].astype(F32) * ma + gb_ref[...].astype(F32) * mb
    x1 = x_ref[...] + _dot(merged.astype(BF16), wout_ref[...])
    h2 = _rms(x1, g2_ref[...]).astype(BF16)
    acc = x1
    ff_chunk = D_MODEL
    for c in range(D_FF // ff_chunk):
        u = jnp.maximum(_dot(h2, wup_ref[:, c * ff_chunk:(c + 1) * ff_chunk]), 0.0)
        acc = acc + _dot((u * u).astype(BF16), wdn_ref[c * ff_chunk:(c + 1) * ff_chunk, :])
    if final:
        acc = _rms(acc, gf_ref[...])
    out_ref[...] = acc


def _post(x, oa, ob, ga, gb, w, gf, tm, final):
    T = x.shape[0]
    row = lambda i: (i, 0)
    in_specs = [
        pl.BlockSpec((tm, D_MODEL), row), pl.BlockSpec((tm, A_WIDTH), row), pl.BlockSpec((tm, B_WIDTH), row),
        pl.BlockSpec((tm, D_MODEL), row), pl.BlockSpec((tm, D_MODEL), row),
        _full_spec(w["wbra"].shape), _full_spec(w["wbrb"].shape), _full_spec(w["wout"].shape),
        _full_spec((1, D_MODEL)), _full_spec(w["wup"].shape), _full_spec(w["wdn"].shape),
        _full_spec((1, D_MODEL)),
    ]
    return pl.pallas_call(
        functools.partial(_post_kernel, final=final),
        grid=(T // tm,), in_specs=in_specs,
        out_specs=pl.BlockSpec((tm, D_MODEL), row),
        out_shape=jax.ShapeDtypeStruct((T, D_MODEL), F32),
        compiler_params=_params(("parallel",)),
        name="post_final" if final else "post",
    )(x, oa, ob, ga, gb, w["wbra"], w["wbrb"], w["wout"], w["g2"], w["wup"], w["wdn"], gf)


def _rot_half(w):
    half = w.shape[-1] // 2
    return jnp.concatenate([-w[..., half:], w[..., :half]], axis=-1)


def _layer_weights(l, lbs, norm1_g, w_in, a_norm_g, q_norm_g, w_uq, kv_norm_g, w_uk, w_uv,
                   w_br_a, w_br_b, w_out, norm2_g, w_up, w_down):
    wi = w_in[l]
    c_b = 4 * A_WIDTH
    c_rope = c_b + Q_LORA + KV_LORA
    c_g = c_rope + B_ROPE
    rope = wi[:, c_rope:c_g]
    wb = jnp.concatenate([wi[:, c_b:c_rope], jnp.tile(rope, (1, B_HEADS)),
                          jnp.tile(_rot_half(rope), (1, B_HEADS))], axis=-1)
    uq = w_uq[l].reshape(Q_LORA, B_HEADS, B_NOPE + B_ROPE)
    nope = uq[:, :, :B_NOPE]
    wqn = jnp.pad(nope, ((0, 0), (0, 0), (0, LANES - B_NOPE))).reshape(Q_LORA, B_HEADS * LANES)
    pe = uq[:, :, B_NOPE:]
    tail = LANES - B_ROPE - B_NOPE
    wqf = jnp.pad(jnp.concatenate([pe, nope], axis=-1), ((0, 0), (0, 0), (0, tail))).reshape(Q_LORA, B_HEADS * LANES)
    wqr = jnp.pad(_rot_half(pe), ((0, 0), (0, 0), (0, LANES - B_ROPE))).reshape(Q_LORA, B_HEADS * LANES)
    lane_pad = ((0, 0), (0, LANES - B_ROPE))
    wbp = jnp.concatenate([wi[:, c_b:c_rope], jnp.pad(rope, lane_pad), jnp.pad(_rot_half(rope), lane_pad)], axis=-1)
    wukf = jnp.pad(w_uk[l], ((0, 0), (0, 0), (B_ROPE, tail)))
    wukf = jnp.swapaxes(wukf, 0, 1).reshape(KV_LORA, B_HEADS * LANES)
    wuvf = jnp.swapaxes(w_uv[l], 1, 2).reshape(B_WIDTH, KV_LORA)
    wqp = jnp.concatenate([pe.reshape(Q_LORA, ROPE_T), _rot_half(pe).reshape(Q_LORA, ROPE_T)], axis=-1)
    wuk = jnp.pad(jnp.swapaxes(w_uk[l], 1, 2), ((0, 0), (0, LANES - B_NOPE), (0, 0)))
    uv = w_uv[l]
    zeros = jnp.zeros_like(uv[0])
    wuv = jnp.stack([jnp.concatenate([jnp.concatenate([uv[2 * p], zeros], axis=-1),
                                      jnp.concatenate([zeros, uv[2 * p + 1]], axis=-1)], axis=0)
                     for p in range(B_HEADS // 2)])
    bf = lambda a: a.astype(BF16)
    return dict(
        g1=norm1_g[l].reshape(1, D_MODEL), wa=bf(wi[:, :c_b]), wb=bf(wb), wg=bf(wi[:, c_g:]),
        qn=q_norm_g[l].reshape(1, Q_LORA), kvn=kv_norm_g[l].reshape(1, KV_LORA),
        wqn=bf(wqn), wqp=bf(wqp), wuk=bf(wuk), wuv=bf(wuv),
        wbp=bf(wbp), wqf=bf(wqf), wqr=bf(wqr), wukf=bf(wukf), wuvf=bf(wuvf),
        lb=lbs[l].reshape(1, A_WIDTH), ang=a_norm_g[l].reshape(1, A_WIDTH),
        wbra=bf(w_br_a[l]), wbrb=bf(w_br_b[l]), wout=bf(w_out[l]),
        g2=norm2_g[l].reshape(1, D_MODEL), wup=bf(w_up[l]), wdn=bf(w_down[l]),
    )


def _rope_angles(pos):
    half = B_ROPE // 2
    inv = ROPE_THETA ** (-jnp.arange(half, dtype=F32) / half)
    ang = pos.astype(F32)[:, None] * inv[None, :]
    return jnp.tile(jnp.cos(ang), (1, 2)), jnp.tile(jnp.sin(ang), (1, 2))


def _rope_tables_tiled(pos):
    cos, sin = _rope_angles(pos)
    return jnp.tile(cos, (1, B_HEADS)), jnp.tile(sin, (1, B_HEADS))


def _rope_tables_block(pos):
    cos, sin = _rope_angles(pos)
    pad = ((0, 0), (0, LANES - B_ROPE))
    return jnp.pad(cos, pad, constant_values=1.0), jnp.pad(sin, pad)


def kernel(x_prompt, x_sample, cache_ckv, cache_krope, state_hgrn, page_table, norm1_g, w_in, lower_bounds,
           a_norm_g, q_norm_g, w_uq, kv_norm_g, w_uk, w_uv, w_br_a, w_br_b, w_out, norm2_g, w_up, w_down,
           final_norm_g):
    B, L, _ = x_prompt.shape
    n_seq, dec, _ = x_sample.shape
    n_logical = page_table.shape[1]
    past_len = n_logical * PAGE_SIZE
    tq = 256
    tm_s = 256
    pages_per_step = min(32, n_logical)
    seqs_per_step = 2

    lbs = _lower_bounds(lower_bounds)
    cos_p, sin_p = _rope_tables_block(jnp.arange(L, dtype=jnp.int32))
    pos_s = past_len + (jnp.arange(tm_s, dtype=jnp.int32) % dec)
    cos_s, sin_s = _rope_tables_tiled(pos_s)
    gf = final_norm_g.reshape(1, D_MODEL)
    cache_krope_t = jnp.swapaxes(cache_krope, 2, 3)

    xp = x_prompt.reshape(B * L, D_MODEL)
    xs = x_sample.reshape(n_seq * dec, D_MODEL)
    outs = [[] for _ in range(6)]
    for l in range(DEPTH):
        w = _layer_weights(l, lbs, norm1_g, w_in, a_norm_g, q_norm_g, w_uq, kv_norm_g, w_uk, w_uv,
                           w_br_a, w_br_b, w_out, norm2_g, w_up, w_down)
        final = l == DEPTH - 1

        qa, fr, v, og, ga, gb, ckv, kpe, qt, kh, vt = _proj_in(xp, w, cos_p, sin_p, tq, True)
        to_seq = lambda a: a.reshape(B, L, A_WIDTH)
        oa, s_p = _hgrn(to_seq(qa), to_seq(fr), to_seq(v), to_seq(og), w["lb"], w["ang"], None, l,
                        nb=1, lblk=256, chunk=64, sub=16)
        ob = _attn_prompt(qt, kh, vt, B, L, tq, tiles=2, heads_per_group=8)
        xp = _post(xp, oa.reshape(B * L, A_WIDTH), ob, ga, gb, w, gf, tq, final)
        outs[0].append(ckv.reshape(B, L, KV_LORA))
        outs[1].append(kpe.reshape(B, L, B_ROPE))
        outs[2].append(s_p)

        qa, fr, v, og, ga, gb, ckv, kpe, qlat, qpe, kcat = _proj_in(xs, w, cos_s, sin_s, tm_s, False)
        to_seq = lambda a: a.reshape(n_seq, dec, A_WIDTH)
        oa, s_s = _hgrn(to_seq(qa), to_seq(fr), to_seq(v), to_seq(og), w["lb"], w["ang"], state_hgrn, l,
                        nb=8, lblk=dec, chunk=dec, sub=dec)
        ob = _attn_sample(page_table, qlat, qpe, kcat, cache_ckv, cache_krope_t, w["wuv"], l, pages_per_step,
                          seqs_per_step)
        xs = _post(xs, oa.reshape(n_seq * dec, A_WIDTH), ob, ga, gb, w, gf, tm_s, final)
        outs[3].append(ckv.reshape(n_seq, dec, KV_LORA))
        outs[4].append(kpe.reshape(n_seq, dec, B_ROPE))
        outs[5].append(s_s)

    return (xp.reshape(B, L, D_MODEL), xs.reshape(n_seq, dec, D_MODEL),
            jnp.stack(outs[0]), jnp.stack(outs[1]), jnp.stack(outs[2]),
            jnp.stack(outs[3]), jnp.stack(outs[4]), jnp.stack(outs[5]))
```

```python
import functools

import jax
import jax.numpy as jnp
from jax import lax
from jax.experimental import pallas as pl
from jax.experimental.pallas import tpu as pltpu

D_MODEL = 1024
DEPTH = 4
PAGE_SIZE = 128
A_HEADS = 4
A_DK = 128
A_DV = 128
A_WIDTH = A_HEADS * A_DV
B_HEADS = 8
B_NOPE = 64
B_ROPE = 32
B_DV = 64
B_WIDTH = B_HEADS * B_DV
Q_LORA = 256
KV_LORA = 256
ROPE_THETA = 10000.0
D_FF = 4 * D_MODEL
EPS = 1e-6
MASK_VALUE = -1e30
F_FLOOR = 1e-30
SM_SCALE = (B_NOPE + B_ROPE) ** -0.5
LOG2E = 1.4426950408889634

LANES = 128
PACKED_ROWS = 16
ROPE_T = B_HEADS * B_ROPE
KCAT = KV_LORA + ROPE_T
VMEM_LIMIT = 56 * 1024 * 1024
PAGE_SLOTS = 3

BF16 = jnp.bfloat16
F32 = jnp.float32

_NT = (((1,), (1,)), ((), ()))


def _dot(a, b):
    return jnp.dot(a, b, preferred_element_type=F32)


def _dot_nt(a, b):
    return lax.dot_general(a, b, _NT, preferred_element_type=F32)


def _full_spec(shape):
    nd = len(shape)
    return pl.BlockSpec(shape, lambda *_: (0,) * nd)


def _params(sem):
    return pltpu.CompilerParams(dimension_semantics=sem, vmem_limit_bytes=VMEM_LIMIT)


def _rms(x, g):
    ms = jnp.mean(x * x, axis=-1, keepdims=True)
    return x * lax.rsqrt(ms + EPS) * g


def _operand_dtype(rows):
    return BF16 if rows >= PACKED_ROWS else F32


def _lbs_kernel(lb_ref, out_ref):
    x = lb_ref[...]
    m = jnp.max(x, axis=0, keepdims=True)
    e = jnp.exp(x - m)
    p = e / jnp.sum(e, axis=0, keepdims=True)
    acc = jnp.zeros_like(p[0:1])
    out_ref[0:1, :] = acc
    for l in range(1, DEPTH):
        acc = acc + p[l:l + 1]
        out_ref[l:l + 1, :] = acc


def _lower_bounds(lower_bounds):
    return pl.pallas_call(
        _lbs_kernel,
        out_shape=jax.ShapeDtypeStruct(lower_bounds.shape, F32),
        name="hgrn_lower_bounds",
    )(lower_bounds.astype(F32))


def _mixer_a_and_gates(h, wa_ref, wg_ref, qa_ref, fr_ref, v_ref, og_ref, ga_ref, gb_ref):
    z = _dot(h, wa_ref[:, 0:A_WIDTH])
    qa_ref[...] = z * jax.nn.sigmoid(z)
    fr_ref[...] = _dot(h, wa_ref[:, A_WIDTH:2 * A_WIDTH])
    v_ref[...] = _dot(h, wa_ref[:, 2 * A_WIDTH:3 * A_WIDTH])
    z = _dot(h, wa_ref[:, 3 * A_WIDTH:4 * A_WIDTH])
    og_ref[...] = z * jax.nn.sigmoid(z)
    half = D_MODEL // 2
    for c in range(2):
        ga_ref[:, c * half:(c + 1) * half] = jax.nn.sigmoid(
            _dot(h, wg_ref[:, c * half:(c + 1) * half])).astype(ga_ref.dtype)
        gb_ref[:, c * half:(c + 1) * half] = jax.nn.sigmoid(
            _dot(h, wg_ref[:, D_MODEL + c * half:D_MODEL + (c + 1) * half])).astype(gb_ref.dtype)


def _proj_in_sample_kernel(x_ref, g1_ref, wa_ref, wb_ref, wg_ref, qn_ref, kvn_ref, wqn_ref, wqp_ref,
                           wuk_ref, cos_ref, sin_ref,
                           qa_ref, fr_ref, v_ref, og_ref, ga_ref, gb_ref, ckv_ref, kpe_ref,
                           qlat_ref, qpe_ref, kcat_ref):
    h = _rms(x_ref[...], g1_ref[...]).astype(BF16)
    _mixer_a_and_gates(h, wa_ref, wg_ref, qa_ref, fr_ref, v_ref, og_ref, ga_ref, gb_ref)

    zb = _dot(h, wb_ref[...])
    cos = cos_ref[...]
    sin = sin_ref[...]
    cq = _rms(zb[:, 0:Q_LORA], qn_ref[...]).astype(BF16)
    ckv = _rms(zb[:, Q_LORA:Q_LORA + KV_LORA], kvn_ref[...])
    off = Q_LORA + KV_LORA
    kpe = zb[:, off:off + ROPE_T] * cos + zb[:, off + ROPE_T:off + 2 * ROPE_T] * sin
    ckv_ref[...] = ckv
    kpe_ref[...] = kpe[:, 0:B_ROPE]
    kcat_ref[:, 0:KV_LORA] = ckv
    kcat_ref[:, KV_LORA:KCAT] = kpe

    qp2 = _dot(cq, wqp_ref[...])
    qpe_ref[...] = (qp2[:, 0:ROPE_T] * cos + qp2[:, ROPE_T:2 * ROPE_T] * sin) * SM_SCALE
    qn = _dot(cq, wqn_ref[...]).astype(BF16)
    for hh in range(B_HEADS):
        qlat_ref[hh] = _dot(qn[:, hh * LANES:(hh + 1) * LANES], wuk_ref[hh]) * SM_SCALE


def _proj_in_prompt_kernel(x_ref, g1_ref, wa_ref, wb_ref, wg_ref, qn_ref, kvn_ref, wqf_ref, wqr_ref,
                           wukf_ref, wuvf_ref, cos_ref, sin_ref,
                           qa_ref, fr_ref, v_ref, og_ref, ga_ref, gb_ref, ckv_ref, kpe_ref,
                           qt_ref, kh_ref, vt_ref):
    h = _rms(x_ref[...], g1_ref[...]).astype(BF16)
    _mixer_a_and_gates(h, wa_ref, wg_ref, qa_ref, fr_ref, v_ref, og_ref, ga_ref, gb_ref)

    zb = _dot(h, wb_ref[...])
    cos = cos_ref[...]
    sin = sin_ref[...]
    cq = _rms(zb[:, 0:Q_LORA], qn_ref[...]).astype(BF16)
    ckv = _rms(zb[:, Q_LORA:Q_LORA + KV_LORA], kvn_ref[...])
    off = Q_LORA + KV_LORA
    kpe = zb[:, off:off + LANES] * cos + zb[:, off + LANES:off + 2 * LANES] * sin
    ckv_ref[...] = ckv
    kpe_ref[...] = kpe[:, 0:B_ROPE]

    kn = _dot(ckv.astype(BF16), wukf_ref[...])
    vt_ref[...] = _dot(wuvf_ref[...], ckv.T.astype(BF16)).astype(BF16)
    zq = _dot(cq, wqf_ref[...])
    zr = _dot(cq, wqr_ref[...])
    for hh in range(B_HEADS):
        blk = slice(hh * LANES, (hh + 1) * LANES)
        kh_ref[hh] = (kn[:, blk] + kpe).astype(BF16)
        qf = (zq[:, blk] * cos + zr[:, blk] * sin) * SM_SCALE
        qt_ref[hh] = qf.T.astype(BF16)


def _proj_in(x, w, cos_tab, sin_tab, tm, prompt):
    T = x.shape[0]
    n_tab = cos_tab.shape[0] // tm
    tab_w = cos_tab.shape[1]
    n_tiles = T // tm
    row = lambda i: (i, 0)
    tab = lambda i: (i % n_tab, 0)
    if prompt:
        weights = [w["wa"], w["wbp"], w["wg"], w["qn"], w["kvn"], w["wqf"], w["wqr"], w["wukf"], w["wuvf"]]
    else:
        weights = [w["wa"], w["wb"], w["wg"], w["qn"], w["kvn"], w["wqn"], w["wqp"], w["wuk"]]
    in_specs = ([pl.BlockSpec((tm, D_MODEL), row), _full_spec((1, D_MODEL))]
                + [_full_spec(a.shape) for a in weights]
                + [pl.BlockSpec((tm, tab_w), tab), pl.BlockSpec((tm, tab_w), tab)])
    out_shape = [
        jax.ShapeDtypeStruct((T, A_WIDTH), F32),
        jax.ShapeDtypeStruct((T, A_WIDTH), F32),
        jax.ShapeDtypeStruct((T, A_WIDTH), F32),
        jax.ShapeDtypeStruct((T, A_WIDTH), F32),
        jax.ShapeDtypeStruct((T, D_MODEL), BF16),
        jax.ShapeDtypeStruct((T, D_MODEL), BF16),
        jax.ShapeDtypeStruct((T, KV_LORA), F32),
        jax.ShapeDtypeStruct((T, B_ROPE), F32),
    ]
    out_specs = [
        pl.BlockSpec((tm, A_WIDTH), row), pl.BlockSpec((tm, A_WIDTH), row),
        pl.BlockSpec((tm, A_WIDTH), row), pl.BlockSpec((tm, A_WIDTH), row),
        pl.BlockSpec((tm, D_MODEL), row), pl.BlockSpec((tm, D_MODEL), row),
        pl.BlockSpec((tm, KV_LORA), row),
        pl.BlockSpec((tm, B_ROPE), row),
    ]
    if prompt:
        tile4 = lambda i: (i, 0, 0, 0)
        out_shape += [
            jax.ShapeDtypeStruct((n_tiles, B_HEADS, LANES, tm), BF16),
            jax.ShapeDtypeStruct((n_tiles, B_HEADS, tm, LANES), BF16),
            jax.ShapeDtypeStruct((n_tiles, B_WIDTH, tm), BF16),
        ]
        out_specs += [pl.BlockSpec((None, B_HEADS, LANES, tm), tile4),
                      pl.BlockSpec((None, B_HEADS, tm, LANES), tile4),
                      pl.BlockSpec((None, B_WIDTH, tm), lambda i: (i, 0, 0))]
    else:
        out_shape += [
            jax.ShapeDtypeStruct((B_HEADS, T, KV_LORA), F32),
            jax.ShapeDtypeStruct((T, ROPE_T), F32),
            jax.ShapeDtypeStruct((T, KCAT), F32),
        ]
        out_specs += [pl.BlockSpec((B_HEADS, tm, KV_LORA), lambda i: (0, i, 0)),
                      pl.BlockSpec((tm, ROPE_T), row),
                      pl.BlockSpec((tm, KCAT), row)]
    return pl.pallas_call(
        _proj_in_prompt_kernel if prompt else _proj_in_sample_kernel,
        grid=(n_tiles,), in_specs=in_specs, out_specs=out_specs, out_shape=out_shape,
        compiler_params=_params(("parallel",)),
        name="proj_in_prompt" if prompt else "proj_in_sample",
    )(x, w["g1"], *weights, cos_tab, sin_tab)


def _cumsum_rows(g):
    n = g.shape[0]
    row = lax.broadcasted_iota(jnp.int32, g.shape, 0)
    b = g
    shift = 1
    while shift < n:
        b = b + jnp.where(row >= shift, pltpu.roll(b, shift, axis=0), 0.0)
        shift *= 2
    return b


def _hgrn_chunk(q, fr, v, lb, st, chunk, sub):
    n_sub = chunk // sub
    md = _operand_dtype(chunk)
    md_sub = _operand_dtype(sub)
    one_m_lb = 1.0 - lb
    f_gate = lb + one_m_lb * jax.nn.sigmoid(fr)
    g = jnp.log(jnp.maximum(f_gate, F_FLOOR))
    k = one_m_lb * jax.nn.sigmoid(-fr)
    b = _cumsum_rows(g)

    o = _dot_nt((q * jnp.exp(b)).astype(md), st.astype(md))
    b_end = b[chunk - 1:chunk, :]
    kd = (k * jnp.exp(b_end - b)).astype(md)
    st_new = st * jnp.exp(b_end) + _dot(v.T.astype(md), kd)

    if n_sub > 1:
        vs = v.astype(md_sub)
        parts = [jnp.zeros((sub, A_DV), F32)]
        for i in range(1, n_sub):
            lo = i * sub
            r_i = b[lo - 1:lo, :]
            q_i = (q[lo:lo + sub] * jnp.exp(b[lo:lo + sub] - r_i)).astype(md_sub)
            k_i = (k[0:lo] * jnp.exp(r_i - b[0:lo])).astype(md_sub)
            a_i = _dot_nt(q_i, k_i).astype(md_sub)
            parts.append(_dot(a_i, vs[0:lo]))
        o = o + jnp.concatenate(parts, axis=0)

    q3 = q.reshape(n_sub, sub, A_DK)
    k3 = k.reshape(n_sub, sub, A_DK)
    b3 = (b * LOG2E).reshape(n_sub, sub, A_DK)
    v3 = v.astype(md).astype(F32).reshape(n_sub, sub, A_DV)
    tpos = lax.broadcasted_iota(jnp.int32, (n_sub, sub, A_DK), 1)
    od = jnp.zeros((n_sub, sub, A_DV), F32)
    for s in range(sub):
        d = jnp.where(tpos >= s, b3 - b3[:, s:s + 1, :], MASK_VALUE)
        wgt = q3 * k3[:, s:s + 1, :] * jnp.exp2(d)
        od = od + jnp.sum(wgt, axis=-1, keepdims=True) * v3[:, s:s + 1, :]
    return o + od.reshape(chunk, A_DV), st_new


def _hgrn_kernel(*refs, chunk, sub, has_s0):
    if has_s0:
        (qa_ref, fr_ref, v_ref, og_ref, lb_ref, ang_ref, s0_ref, oa_ref, sout_ref, st_ref) = refs
    else:
        (qa_ref, fr_ref, v_ref, og_ref, lb_ref, ang_ref, oa_ref, sout_ref, st_ref) = refs
        s0_ref = None
    j = pl.program_id(1)
    nb, lblk = qa_ref.shape[0], qa_ref.shape[1]
    n_chunks = lblk // chunk

    @pl.when(j == 0)
    def _():
        if has_s0:
            for n in range(nb):
                for hh in range(A_HEADS):
                    st_ref[n, hh] = s0_ref[n, hh].T
        else:
            st_ref[...] = jnp.zeros_like(st_ref)

    def seq_body(n, carry):
        def chunk_body(c, carry2):
            rows = pl.ds(pl.multiple_of(c * chunk, chunk), chunk)
            for hh in range(A_HEADS):
                cols = slice(hh * LANES, (hh + 1) * LANES)
                o, st_new = _hgrn_chunk(qa_ref[n, rows, cols], fr_ref[n, rows, cols], v_ref[n, rows, cols],
                                        lb_ref[:, cols], st_ref[n, hh], chunk, sub)
                st_ref[n, hh] = st_new
                on = _rms(o, ang_ref[:, cols])
                oa_ref[n, rows, cols] = (on * og_ref[n, rows, cols]).astype(oa_ref.dtype)
            return carry2
        return lax.fori_loop(0, n_chunks, chunk_body, carry, unroll=min(4, n_chunks))
    lax.fori_loop(0, nb, seq_body, 0)

    @pl.when(j == pl.num_programs(1) - 1)
    def _():
        for n in range(nb):
            for hh in range(A_HEADS):
                sout_ref[n, hh] = st_ref[n, hh].T


def _hgrn(qa, fr, v, og, lb, ang, s0, layer, nb, lblk, chunk, sub):
    B, L, _ = qa.shape
    has_s0 = s0 is not None
    grid = (B // nb, L // lblk)
    act = pl.BlockSpec((nb, lblk, A_WIDTH), lambda i, j: (i, j, 0))
    st_spec = pl.BlockSpec((nb, A_HEADS, A_DK, A_DV), lambda i, j: (i, 0, 0, 0))
    in_specs = [act, act, act, act, _full_spec((1, A_WIDTH)), _full_spec((1, A_WIDTH))]
    args = [qa, fr, v, og, lb, ang]
    if has_s0:
        in_specs.append(pl.BlockSpec((None, nb, A_HEADS, A_DK, A_DV), lambda i, j: (layer, i, 0, 0, 0)))
        args.append(s0)
    return pl.pallas_call(
        functools.partial(_hgrn_kernel, chunk=chunk, sub=sub, has_s0=has_s0),
        grid=grid, in_specs=in_specs,
        out_specs=[act, st_spec],
        out_shape=[jax.ShapeDtypeStruct((B, L, A_WIDTH), BF16),
                   jax.ShapeDtypeStruct((B, A_HEADS, A_DK, A_DV), F32)],
        scratch_shapes=[pltpu.VMEM((nb, A_HEADS, A_DV, A_DK), F32)],
        compiler_params=_params(("parallel", "arbitrary")),
        name="hgrn_sample" if has_s0 else "hgrn_prompt",
    )(*args)


def _attn_prompt_kernel(qt_ref, kh_ref, vt_ref, out_ref, m_s, l_s, acc_s, *, tm, tiles, heads_per_group):
    qi = pl.program_id(1)
    tq = tiles * tm
    gw = heads_per_group * tq
    m_s[...] = jnp.full_like(m_s, -jnp.inf)
    l_s[...] = jnp.zeros_like(l_s)
    acc_s[...] = jnp.zeros_like(acc_s)

    def tile(kt, diagonal):
        vt = jnp.concatenate([vt_ref[kt * tiles + j] for j in range(tiles)], axis=1)
        for grp in range(B_HEADS // heads_per_group):
            heads = range(grp * heads_per_group, (grp + 1) * heads_per_group)
            gcols = slice(grp * gw, (grp + 1) * gw)
            parts = []
            for hh in heads:
                keys = jnp.concatenate([kh_ref[kt * tiles + j, hh] for j in range(tiles)], axis=0)
                qry = jnp.concatenate([qt_ref[j, hh] for j in range(tiles)], axis=1)
                parts.append(_dot(keys, qry))
            st = jnp.concatenate(parts, axis=1)
            if diagonal:
                kpos = lax.broadcasted_iota(jnp.int32, st.shape, 0)
                qpos = lax.broadcasted_iota(jnp.int32, st.shape, 1) % tq
                st = jnp.where(kpos <= qpos, st, MASK_VALUE)
            m_prev = m_s[:, gcols]
            m_new = jnp.maximum(m_prev, jnp.max(st, axis=0, keepdims=True))
            alpha = jnp.exp(m_prev - m_new)
            p = jnp.exp(st - m_new)
            l_s[:, gcols] = alpha * l_s[:, gcols] + jnp.sum(p, axis=0, keepdims=True)
            m_s[:, gcols] = m_new
            pb = p.astype(BF16)
            for i, hh in enumerate(heads):
                cols = slice(i * tq, (i + 1) * tq)
                rows = slice(hh * B_DV, (hh + 1) * B_DV)
                acc_s[rows, :] = alpha[:, cols] * acc_s[rows, :] + _dot(vt[rows, :], pb[:, cols])

    def body(kt, carry):
        tile(kt, False)
        return carry
    lax.fori_loop(0, qi, body, 0)
    tile(qi, True)

    o_t = jnp.concatenate([acc_s[hh * B_DV:(hh + 1) * B_DV, :] / l_s[:, hh * tq:(hh + 1) * tq]
                           for hh in range(B_HEADS)], axis=0)
    out_ref[...] = o_t.T.astype(out_ref.dtype)


def _attn_prompt(qt, kh, vt, B, L, tm, tiles, heads_per_group):
    tq = tiles * tm
    nq = L // tq
    n_tm = L // tm
    return pl.pallas_call(
        functools.partial(_attn_prompt_kernel, tm=tm, tiles=tiles, heads_per_group=heads_per_group),
        grid=(B, nq),
        in_specs=[
            pl.BlockSpec((tiles, B_HEADS, LANES, tm), lambda b, i: (b * nq + i, 0, 0, 0)),
            pl.BlockSpec((n_tm, B_HEADS, tm, LANES), lambda b, i: (b, 0, 0, 0)),
            pl.BlockSpec((n_tm, B_WIDTH, tm), lambda b, i: (b, 0, 0)),
        ],
        out_specs=pl.BlockSpec((tq, B_WIDTH), lambda b, i: (b * nq + i, 0)),
        out_shape=jax.ShapeDtypeStruct((B * L, B_WIDTH), BF16),
        scratch_shapes=[pltpu.VMEM((1, B_HEADS * tq), F32), pltpu.VMEM((1, B_HEADS * tq), F32),
                        pltpu.VMEM((B_WIDTH, tq), F32)],
        compiler_params=_params(("parallel", "arbitrary")),
        name="attn_prompt",
    )(qt, kh, vt)


def _lane_tile(x, width):
    if width % LANES == 0:
        return jnp.concatenate([x] * (width // LANES), axis=1) if width > LANES else x
    return x[:, 0:width]


def _attn_sample_kernel(pt_ref, qlat_ref, qpe_ref, knew_ref, wuv_ref, ckv_hbm, krt_hbm, out_ref,
                        q_s, qp_s, k_s, kt_s, kpt_s, m_s, l_s, acc_s, kbuf, krbuf, sem,
                        *, layer, n_pages, n_seq_blk, dec):
    blk = pl.program_id(0)
    g = pl.program_id(1)
    n_groups = pl.num_programs(1)
    step = blk * n_groups + g
    n_steps = pl.num_programs(0) * n_groups
    slot = lax.rem(step, PAGE_SLOTS)
    rows = B_HEADS * dec

    def page_copies(slot_, page_of):
        out = []
        for sq in range(n_seq_blk):
            for i in range(n_pages):
                page = page_of(sq, i)
                keys = pl.ds(i * PAGE_SIZE, PAGE_SIZE)
                out.append(pltpu.make_async_copy(ckv_hbm.at[layer, page], kbuf.at[slot_, sq, keys, :], sem.at[slot_]))
                out.append(pltpu.make_async_copy(krt_hbm.at[layer, page], krbuf.at[slot_, sq, :, keys], sem.at[slot_]))
        return out

    def start_step(step_, slot_):
        blk_ = step_ // n_groups
        g_ = step_ - blk_ * n_groups
        for c in page_copies(slot_, lambda sq, i: pt_ref[blk_ * n_seq_blk + sq, g_ * n_pages + i]):
            c.start()

    ahead = PAGE_SLOTS - 1

    @pl.when(step == 0)
    def _():
        for d in range(ahead):
            @pl.when(d < n_steps)
            def _():
                start_step(d, d)

    @pl.when(step + ahead < n_steps)
    def _():
        start_step(step + ahead, lax.rem(step + ahead, PAGE_SLOTS))

    for c in page_copies(slot, lambda sq, i: 0):
        c.wait()

    @pl.when(g == 0)
    def _():
        for sq in range(n_seq_blk):
            seq = blk * n_seq_blk + sq
            q_s[sq] = qlat_ref[:, seq].reshape(rows, KV_LORA).astype(BF16)
            qpe = qpe_ref[seq]
            qp_s[sq] = jnp.concatenate(
                [qpe[:, hh * B_ROPE:(hh + 1) * B_ROPE] for hh in range(B_HEADS)], axis=0).astype(BF16)
        m_s[...] = jnp.full_like(m_s, -jnp.inf)
        l_s[...] = jnp.zeros_like(l_s)
        acc_s[...] = jnp.zeros_like(acc_s)

    def update(sq, s, vals):
        n = s.shape[1]
        m_prev = m_s[sq]
        m_new = jnp.maximum(m_prev, jnp.max(s, axis=-1, keepdims=True))
        alpha = jnp.exp(m_prev - m_new)
        p = jnp.exp(s - _lane_tile(m_new, n))
        l_s[sq] = alpha * l_s[sq] + jnp.sum(p, axis=-1, keepdims=True)
        acc_s[sq] = _lane_tile(alpha, KV_LORA) * acc_s[sq] + _dot(p.astype(vals.dtype), vals)
        m_s[sq] = m_new

    for sq in range(n_seq_blk):
        for i in range(n_pages):
            page = kbuf[slot, sq, i * PAGE_SIZE:(i + 1) * PAGE_SIZE, :]
            k_s[sq, i * PAGE_SIZE:(i + 1) * PAGE_SIZE, :] = page.astype(BF16)
            kt_s[sq, :, i * PAGE_SIZE:(i + 1) * PAGE_SIZE] = page.T.astype(BF16)
        kpt_s[sq] = krbuf[slot, sq].astype(BF16)
        kc = k_s[sq]
        update(sq, _dot(q_s[sq], kt_s[sq]) + _dot(qp_s[sq], kpt_s[sq]), kc)

    @pl.when(g == pl.num_programs(1) - 1)
    def _():
        qpos = lax.broadcasted_iota(jnp.int32, (rows, dec), 0) % dec
        kpos = lax.broadcasted_iota(jnp.int32, (rows, dec), 1)
        for sq in range(n_seq_blk):
            seq = blk * n_seq_blk + sq
            knew = knew_ref[seq]
            kc_new = knew[:, 0:KV_LORA]
            kp_new = knew[:, KV_LORA:KV_LORA + B_ROPE]
            s = _dot_nt(q_s[sq].astype(F32), kc_new) + _dot_nt(qp_s[sq].astype(F32), kp_new)
            update(sq, jnp.where(kpos <= qpos, s, MASK_VALUE), kc_new)
            o = (acc_s[sq] / _lane_tile(l_s[sq], KV_LORA)).astype(_operand_dtype(dec))
            for p in range(B_HEADS // 2):
                pair = jnp.concatenate([o[(2 * p) * dec:(2 * p + 1) * dec],
                                        o[(2 * p + 1) * dec:(2 * p + 2) * dec]], axis=-1)
                out_ref[seq, :, p * LANES:(p + 1) * LANES] = _dot(pair, wuv_ref[p].astype(pair.dtype))


def _attn_sample(page_table, qlat, qpe, kcat, cache_ckv, cache_krope_t, wuv, layer, n_pages, n_seq_blk):
    n_seq, n_logical = page_table.shape
    dec = qlat.shape[1] // n_seq
    rows = B_HEADS * dec
    groups = n_logical // n_pages

    whole3 = lambda n, g, pt: (0, 0, 0)
    in_specs = [
        pl.BlockSpec((B_HEADS, n_seq, dec, KV_LORA), lambda n, g, pt: (0, 0, 0, 0)),
        pl.BlockSpec((n_seq, dec, ROPE_T), whole3),
        pl.BlockSpec((n_seq, dec, KCAT), whole3),
        pl.BlockSpec(wuv.shape, whole3),
        pl.BlockSpec(memory_space=pl.ANY),
        pl.BlockSpec(memory_space=pl.ANY),
    ]
    keys = n_pages * PAGE_SIZE
    grid_spec = pltpu.PrefetchScalarGridSpec(
        num_scalar_prefetch=1, grid=(n_seq // n_seq_blk, groups), in_specs=in_specs,
        out_specs=pl.BlockSpec((n_seq, dec, B_WIDTH), whole3),
        scratch_shapes=[pltpu.VMEM((n_seq_blk, rows, KV_LORA), BF16), pltpu.VMEM((n_seq_blk, rows, B_ROPE), BF16),
                        pltpu.VMEM((n_seq_blk, keys, KV_LORA), BF16),
                        pltpu.VMEM((n_seq_blk, KV_LORA, keys), BF16),
                        pltpu.VMEM((n_seq_blk, B_ROPE, keys), BF16),
                        pltpu.VMEM((n_seq_blk, rows, LANES), F32), pltpu.VMEM((n_seq_blk, rows, LANES), F32),
                        pltpu.VMEM((n_seq_blk, rows, KV_LORA), F32),
                        pltpu.VMEM((PAGE_SLOTS, n_seq_blk, keys, KV_LORA), F32),
                        pltpu.VMEM((PAGE_SLOTS, n_seq_blk, B_ROPE, keys), F32),
                        pltpu.SemaphoreType.DMA((PAGE_SLOTS,))])
    out = pl.pallas_call(
        functools.partial(_attn_sample_kernel, layer=layer, n_pages=n_pages, n_seq_blk=n_seq_blk, dec=dec),
        grid_spec=grid_spec,
        out_shape=jax.ShapeDtypeStruct((n_seq, dec, B_WIDTH), F32),
        compiler_params=_params(("arbitrary", "arbitrary")),
        name="attn_sample",
    )(page_table, qlat.reshape(B_HEADS, n_seq, dec, KV_LORA), qpe.reshape(n_seq, dec, ROPE_T),
      kcat.reshape(n_seq, dec, KCAT), wuv, cache_ckv, cache_krope_t)
    return out.reshape(n_seq * dec, B_WIDTH)


def _post_kernel(x_ref, oa_ref, ob_ref, ga_ref, gb_ref, wbra_ref, wbrb_ref, wout_ref, g2_ref,
                 wup_ref, wdn_ref, gf_ref, out_ref, *, final):
    ma = _dot(oa_ref[...].astype(BF16), wbra_ref[...])
    mb = _dot(ob_ref[...].astype(BF16), wbrb_ref[...])
    merged = ga_ref[...].astype(F32) * ma + gb_ref[...].astype(F32) * mb
    x1 = x_ref[...] + _dot(merged.astype(BF16), wout_ref[...])
    h2 = _rms(x1, g2_ref[...]).astype(BF16)
    acc = x1
    ff_chunk = D_MODEL
    for c in range(D_FF // ff_chunk):
        u = jnp.maximum(_dot(h2, wup_ref[:, c * ff_chunk:(c + 1) * ff_chunk]), 0.0)
        acc = acc + _dot((u * u).astype(BF16), wdn_ref[c * ff_chunk:(c + 1) * ff_chunk, :])
    if final:
        acc = _rms(acc, gf_ref[...])
    out_ref[...] = acc


def _layer_spec(stacked, layer):
    return pl.BlockSpec((None,) + stacked.shape[1:], lambda i: (layer, 0, 0), pipeline_mode=pl.Buffered(1))


def _post(x, oa, ob, ga, gb, w, ws, layer, gf, tm, final):
    T = x.shape[0]
    row = lambda i: (i, 0)
    stacks = [ws["wbra"], ws["wbrb"], ws["wout"]]
    in_specs = [
        pl.BlockSpec((tm, D_MODEL), row), pl.BlockSpec((tm, A_WIDTH), row), pl.BlockSpec((tm, B_WIDTH), row),
        pl.BlockSpec((tm, D_MODEL), row), pl.BlockSpec((tm, D_MODEL), row),
        _layer_spec(ws["wbra"], layer), _layer_spec(ws["wbrb"], layer), _layer_spec(ws["wout"], layer),
        _full_spec((1, D_MODEL)), _layer_spec(ws["wup"], layer), _layer_spec(ws["wdn"], layer),
        _full_spec((1, D_MODEL)),
    ]
    return pl.pallas_call(
        functools.partial(_post_kernel, final=final),
        grid=(T // tm,), in_specs=in_specs,
        out_specs=pl.BlockSpec((tm, D_MODEL), row),
        out_shape=jax.ShapeDtypeStruct((T, D_MODEL), F32),
        compiler_params=_params(("parallel",)),
        name="post_final" if final else "post",
    )(x, oa, ob, ga, gb, *stacks, w["g2"], ws["wup"], ws["wdn"], gf)


def _rot_half(w):
    half = w.shape[-1] // 2
    return jnp.concatenate([-w[..., half:], w[..., :half]], axis=-1)


def _layer_weights(l, lbs, norm1_g, w_in, a_norm_g, q_norm_g, w_uq, kv_norm_g, w_uk, w_uv,
                   w_br_a, w_br_b, w_out, norm2_g, w_up, w_down):
    wi = w_in[l]
    c_b = 4 * A_WIDTH
    c_rope = c_b + Q_LORA + KV_LORA
    c_g = c_rope + B_ROPE
    rope = wi[:, c_rope:c_g]
    wb = jnp.concatenate([wi[:, c_b:c_rope], jnp.tile(rope, (1, B_HEADS)),
                          jnp.tile(_rot_half(rope), (1, B_HEADS))], axis=-1)
    uq = w_uq[l].reshape(Q_LORA, B_HEADS, B_NOPE + B_ROPE)
    nope = uq[:, :, :B_NOPE]
    wqn = jnp.pad(nope, ((0, 0), (0, 0), (0, LANES - B_NOPE))).reshape(Q_LORA, B_HEADS * LANES)
    pe = uq[:, :, B_NOPE:]
    tail = LANES - B_ROPE - B_NOPE
    wqf = jnp.pad(jnp.concatenate([pe, nope], axis=-1), ((0, 0), (0, 0), (0, tail))).reshape(Q_LORA, B_HEADS * LANES)
    wqr = jnp.pad(_rot_half(pe), ((0, 0), (0, 0), (0, LANES - B_ROPE))).reshape(Q_LORA, B_HEADS * LANES)
    lane_pad = ((0, 0), (0, LANES - B_ROPE))
    wbp = jnp.concatenate([wi[:, c_b:c_rope], jnp.pad(rope, lane_pad), jnp.pad(_rot_half(rope), lane_pad)], axis=-1)
    wukf = jnp.pad(w_uk[l], ((0, 0), (0, 0), (B_ROPE, tail)))
    wukf = jnp.swapaxes(wukf, 0, 1).reshape(KV_LORA, B_HEADS * LANES)
    wuvf = jnp.swapaxes(w_uv[l], 1, 2).reshape(B_WIDTH, KV_LORA)
    wqp = jnp.concatenate([pe.reshape(Q_LORA, ROPE_T), _rot_half(pe).reshape(Q_LORA, ROPE_T)], axis=-1)
    wuk = jnp.pad(jnp.swapaxes(w_uk[l], 1, 2), ((0, 0), (0, LANES - B_NOPE), (0, 0)))
    uv = w_uv[l]
    zeros = jnp.zeros_like(uv[0])
    wuv = jnp.stack([jnp.concatenate([jnp.concatenate([uv[2 * p], zeros], axis=-1),
                                      jnp.concatenate([zeros, uv[2 * p + 1]], axis=-1)], axis=0)
                     for p in range(B_HEADS // 2)])
    bf = lambda a: a.astype(BF16)
    return dict(
        g1=norm1_g[l].reshape(1, D_MODEL), wa=bf(wi[:, :c_b]), wb=bf(wb), wg=bf(wi[:, c_g:]),
        qn=q_norm_g[l].reshape(1, Q_LORA), kvn=kv_norm_g[l].reshape(1, KV_LORA),
        wqn=bf(wqn), wqp=bf(wqp), wuk=bf(wuk), wuv=bf(wuv),
        wbp=bf(wbp), wqf=bf(wqf), wqr=bf(wqr), wukf=bf(wukf), wuvf=bf(wuvf),
        lb=lbs[l].reshape(1, A_WIDTH), ang=a_norm_g[l].reshape(1, A_WIDTH),
        g2=norm2_g[l].reshape(1, D_MODEL),
    )


def _rope_angles(pos):
    half = B_ROPE // 2
    inv = ROPE_THETA ** (-jnp.arange(half, dtype=F32) / half)
    ang = pos.astype(F32)[:, None] * inv[None, :]
    return jnp.tile(jnp.cos(ang), (1, 2)), jnp.tile(jnp.sin(ang), (1, 2))


def _rope_tables_tiled(pos):
    cos, sin = _rope_angles(pos)
    return jnp.tile(cos, (1, B_HEADS)), jnp.tile(sin, (1, B_HEADS))


def _rope_tables_block(pos):
    cos, sin = _rope_angles(pos)
    pad = ((0, 0), (0, LANES - B_ROPE))
    return jnp.pad(cos, pad, constant_values=1.0), jnp.pad(sin, pad)


def kernel(x_prompt, x_sample, cache_ckv, cache_krope, state_hgrn, page_table, norm1_g, w_in, lower_bounds,
           a_norm_g, q_norm_g, w_uq, kv_norm_g, w_uk, w_uv, w_br_a, w_br_b, w_out, norm2_g, w_up, w_down,
           final_norm_g):
    B, L, _ = x_prompt.shape
    n_seq, dec, _ = x_sample.shape
    n_logical = page_table.shape[1]
    past_len = n_logical * PAGE_SIZE
    tq = 256
    tm_s = 256
    pages_per_step = min(32, n_logical)
    seqs_per_step = 2

    lbs = _lower_bounds(lower_bounds)
    cos_p, sin_p = _rope_tables_block(jnp.arange(L, dtype=jnp.int32))
    pos_s = past_len + (jnp.arange(tm_s, dtype=jnp.int32) % dec)
    cos_s, sin_s = _rope_tables_tiled(pos_s)
    gf = final_norm_g.reshape(1, D_MODEL)
    cache_krope_t = jnp.swapaxes(cache_krope, 2, 3)

    xp = x_prompt.reshape(B * L, D_MODEL)
    xs = x_sample.reshape(n_seq * dec, D_MODEL)
    outs = [[] for _ in range(6)]
    w_in_b = w_in.astype(BF16)
    ws = dict(wbra=w_br_a.astype(BF16), wbrb=w_br_b.astype(BF16), wout=w_out.astype(BF16),
              wup=w_up.astype(BF16), wdn=w_down.astype(BF16))
    for l in range(DEPTH):
        w = _layer_weights(l, lbs, norm1_g, w_in_b, a_norm_g, q_norm_g, w_uq, kv_norm_g, w_uk, w_uv,
                           w_br_a, w_br_b, w_out, norm2_g, w_up, w_down)
        final = l == DEPTH - 1

        qa, fr, v, og, ga, gb, ckv, kpe, qt, kh, vt = _proj_in(xp, w, cos_p, sin_p, tq, True)
        to_seq = lambda a: a.reshape(B, L, A_WIDTH)
        oa, s_p = _hgrn(to_seq(qa), to_seq(fr), to_seq(v), to_seq(og), w["lb"], w["ang"], None, l,
                        nb=1, lblk=256, chunk=64, sub=16)
        ob = _attn_prompt(qt, kh, vt, B, L, tq, tiles=2, heads_per_group=8)
        xp = _post(xp, oa.reshape(B * L, A_WIDTH), ob, ga, gb, w, ws, l, gf, tq, final)
        outs[0].append(ckv.reshape(B, L, KV_LORA))
        outs[1].append(kpe.reshape(B, L, B_ROPE))
        outs[2].append(s_p)

        qa, fr, v, og, ga, gb, ckv, kpe, qlat, qpe, kcat = _proj_in(xs, w, cos_s, sin_s, tm_s, False)
        to_seq = lambda a: a.reshape(n_seq, dec, A_WIDTH)
        oa, s_s = _hgrn(to_seq(qa), to_seq(fr), to_seq(v), to_seq(og), w["lb"], w["ang"], state_hgrn, l,
                        nb=8, lblk=dec, chunk=dec, sub=dec)
        ob = _attn_sample(page_table, qlat, qpe, kcat, cache_ckv, cache_krope_t, w["wuv"], l, pages_per_step,
                          seqs_per_step)
        xs = _post(xs, oa.reshape(n_seq * dec, A_WIDTH), ob, ga, gb, w, ws, l, gf, tm_s, final)
        outs[3].append(ckv.reshape(n_seq, dec, KV_LORA))
        outs[4].append(kpe.reshape(n_seq, dec, B_ROPE))
        outs[5].append(s_s)

    return (xp.reshape(B, L, D_MODEL), xs.reshape(n_seq, dec, D_MODEL),
            jnp.stack(outs[0]), jnp.stack(outs[1]), jnp.stack(outs[2]),
            jnp.stack(outs[3]), jnp.stack(outs[4]), jnp.stack(outs[5]))
```

```python
import functools

import jax
import jax.numpy as jnp
from jax import lax
from jax.experimental import pallas as pl
from jax.experimental.pallas import tpu as pltpu

D_MODEL = 1024
DEPTH = 4
PAGE_SIZE = 128
A_HEADS = 4
A_DK = 128
A_DV = 128
A_WIDTH = A_HEADS * A_DV
B_HEADS = 8
B_NOPE = 64
B_ROPE = 32
B_DV = 64
B_WIDTH = B_HEADS * B_DV
Q_LORA = 256
KV_LORA = 256
ROPE_THETA = 10000.0
D_FF = 4 * D_MODEL
EPS = 1e-6
MASK_VALUE = -1e30
F_FLOOR = 1e-30
SM_SCALE = (B_NOPE + B_ROPE) ** -0.5
LOG2E = 1.4426950408889634

LANES = 128
PACKED_ROWS = 16
ROPE_T = B_HEADS * B_ROPE
KCAT = KV_LORA + ROPE_T
VMEM_LIMIT = 56 * 1024 * 1024
PAGE_SLOTS = 3

BF16 = jnp.bfloat16
F32 = jnp.float32

_NT = (((1,), (1,)), ((), ()))


def _dot(a, b):
    return jnp.dot(a, b, preferred_element_type=F32)


def _dot_nt(a, b):
    return lax.dot_general(a, b, _NT, preferred_element_type=F32)


def _full_spec(shape):
    nd = len(shape)
    return pl.BlockSpec(shape, lambda *_: (0,) * nd)


def _params(sem):
    return pltpu.CompilerParams(dimension_semantics=sem, vmem_limit_bytes=VMEM_LIMIT)


def _rms(x, g):
    ms = jnp.mean(x * x, axis=-1, keepdims=True)
    return x * lax.rsqrt(ms + EPS) * g


def _operand_dtype(rows):
    return BF16 if rows >= PACKED_ROWS else F32


def _lbs_kernel(lb_ref, out_ref):
    x = lb_ref[...]
    m = jnp.max(x, axis=0, keepdims=True)
    e = jnp.exp(x - m)
    p = e / jnp.sum(e, axis=0, keepdims=True)
    acc = jnp.zeros_like(p[0:1])
    out_ref[0:1, :] = acc
    for l in range(1, DEPTH):
        acc = acc + p[l:l + 1]
        out_ref[l:l + 1, :] = acc


def _lower_bounds(lower_bounds):
    return pl.pallas_call(
        _lbs_kernel,
        out_shape=jax.ShapeDtypeStruct(lower_bounds.shape, F32),
        name="hgrn_lower_bounds",
    )(lower_bounds.astype(F32))


def _mixer_a_and_gates(h, wa_ref, wg_ref, qa_ref, fr_ref, v_ref, og_ref, ga_ref, gb_ref):
    z = _dot(h, wa_ref[:, 0:A_WIDTH])
    qa_ref[...] = z * jax.nn.sigmoid(z)
    fr_ref[...] = _dot(h, wa_ref[:, A_WIDTH:2 * A_WIDTH])
    v_ref[...] = _dot(h, wa_ref[:, 2 * A_WIDTH:3 * A_WIDTH])
    z = _dot(h, wa_ref[:, 3 * A_WIDTH:4 * A_WIDTH])
    og_ref[...] = z * jax.nn.sigmoid(z)
    half = D_MODEL // 2
    for c in range(2):
        ga_ref[:, c * half:(c + 1) * half] = jax.nn.sigmoid(
            _dot(h, wg_ref[:, c * half:(c + 1) * half])).astype(ga_ref.dtype)
        gb_ref[:, c * half:(c + 1) * half] = jax.nn.sigmoid(
            _dot(h, wg_ref[:, D_MODEL + c * half:D_MODEL + (c + 1) * half])).astype(gb_ref.dtype)


def _proj_in_sample_kernel(x_ref, g1_ref, wa_ref, wb_ref, wg_ref, qn_ref, kvn_ref, wqn_ref, wqp_ref,
                           wuk_ref, cos_ref, sin_ref,
                           qa_ref, fr_ref, v_ref, og_ref, ga_ref, gb_ref, ckv_ref, kpe_ref,
                           qlat_ref, qpe_ref, kcat_ref):
    h = _rms(x_ref[...], g1_ref[...]).astype(BF16)
    _mixer_a_and_gates(h, wa_ref, wg_ref, qa_ref, fr_ref, v_ref, og_ref, ga_ref, gb_ref)

    zb = _dot(h, wb_ref[...])
    cos = cos_ref[...]
    sin = sin_ref[...]
    cq = _rms(zb[:, 0:Q_LORA], qn_ref[...]).astype(BF16)
    ckv = _rms(zb[:, Q_LORA:Q_LORA + KV_LORA], kvn_ref[...])
    off = Q_LORA + KV_LORA
    kpe = zb[:, off:off + ROPE_T] * cos + zb[:, off + ROPE_T:off + 2 * ROPE_T] * sin
    ckv_ref[...] = ckv
    kpe_ref[...] = kpe[:, 0:B_ROPE]
    kcat_ref[:, 0:KV_LORA] = ckv
    kcat_ref[:, KV_LORA:KCAT] = kpe

    qp2 = _dot(cq, wqp_ref[...])
    qpe_ref[...] = (qp2[:, 0:ROPE_T] * cos + qp2[:, ROPE_T:2 * ROPE_T] * sin) * SM_SCALE
    qn = _dot(cq, wqn_ref[...]).astype(BF16)
    for hh in range(B_HEADS):
        qlat_ref[hh] = _dot(qn[:, hh * LANES:(hh + 1) * LANES], wuk_ref[hh]) * SM_SCALE


def _proj_in_prompt_kernel(x_ref, g1_ref, wa_ref, wb_ref, wg_ref, qn_ref, kvn_ref, wqf_ref, wqr_ref,
                           wukf_ref, wuvf_ref, cos_ref, sin_ref,
                           qa_ref, fr_ref, v_ref, og_ref, ga_ref, gb_ref, ckv_ref, kpe_ref,
                           qt_ref, kh_ref, vt_ref):
    h = _rms(x_ref[...], g1_ref[...]).astype(BF16)
    _mixer_a_and_gates(h, wa_ref, wg_ref, qa_ref, fr_ref, v_ref, og_ref, ga_ref, gb_ref)

    zb = _dot(h, wb_ref[...])
    cos = cos_ref[...]
    sin = sin_ref[...]
    cq = _rms(zb[:, 0:Q_LORA], qn_ref[...]).astype(BF16)
    ckv = _rms(zb[:, Q_LORA:Q_LORA + KV_LORA], kvn_ref[...])
    off = Q_LORA + KV_LORA
    kpe = zb[:, off:off + LANES] * cos + zb[:, off + LANES:off + 2 * LANES] * sin
    ckv_ref[...] = ckv
    kpe_ref[...] = kpe[:, 0:B_ROPE]

    kn = _dot(ckv.astype(BF16), wukf_ref[...])
    vt_ref[...] = _dot(wuvf_ref[...], ckv.T.astype(BF16)).astype(BF16)
    zq = _dot(cq, wqf_ref[...])
    zr = _dot(cq, wqr_ref[...])
    for hh in range(B_HEADS):
        blk = slice(hh * LANES, (hh + 1) * LANES)
        kh_ref[hh] = (kn[:, blk] + kpe).astype(BF16)
        qf = (zq[:, blk] * cos + zr[:, blk] * sin) * SM_SCALE
        qt_ref[hh] = qf.T.astype(BF16)


def _proj_in(x, w, cos_tab, sin_tab, tm, prompt):
    T = x.shape[0]
    n_tab = cos_tab.shape[0] // tm
    tab_w = cos_tab.shape[1]
    n_tiles = T // tm
    row = lambda i: (i, 0)
    tab = lambda i: (i % n_tab, 0)
    if prompt:
        weights = [w["wa"], w["wbp"], w["wg"], w["qn"], w["kvn"], w["wqf"], w["wqr"], w["wukf"], w["wuvf"]]
    else:
        weights = [w["wa"], w["wb"], w["wg"], w["qn"], w["kvn"], w["wqn"], w["wqp"], w["wuk"]]
    in_specs = ([pl.BlockSpec((tm, D_MODEL), row), _full_spec((1, D_MODEL))]
                + [_full_spec(a.shape) for a in weights]
                + [pl.BlockSpec((tm, tab_w), tab), pl.BlockSpec((tm, tab_w), tab)])
    out_shape = [
        jax.ShapeDtypeStruct((T, A_WIDTH), F32),
        jax.ShapeDtypeStruct((T, A_WIDTH), F32),
        jax.ShapeDtypeStruct((T, A_WIDTH), F32),
        jax.ShapeDtypeStruct((T, A_WIDTH), F32),
        jax.ShapeDtypeStruct((T, D_MODEL), BF16),
        jax.ShapeDtypeStruct((T, D_MODEL), BF16),
        jax.ShapeDtypeStruct((T, KV_LORA), F32),
        jax.ShapeDtypeStruct((T, B_ROPE), F32),
    ]
    out_specs = [
        pl.BlockSpec((tm, A_WIDTH), row), pl.BlockSpec((tm, A_WIDTH), row),
        pl.BlockSpec((tm, A_WIDTH), row), pl.BlockSpec((tm, A_WIDTH), row),
        pl.BlockSpec((tm, D_MODEL), row), pl.BlockSpec((tm, D_MODEL), row),
        pl.BlockSpec((tm, KV_LORA), row),
        pl.BlockSpec((tm, B_ROPE), row),
    ]
    if prompt:
        tile4 = lambda i: (i, 0, 0, 0)
        out_shape += [
            jax.ShapeDtypeStruct((n_tiles, B_HEADS, LANES, tm), BF16),
            jax.ShapeDtypeStruct((n_tiles, B_HEADS, tm, LANES), BF16),
            jax.ShapeDtypeStruct((n_tiles, B_WIDTH, tm), BF16),
        ]
        out_specs += [pl.BlockSpec((None, B_HEADS, LANES, tm), tile4),
                      pl.BlockSpec((None, B_HEADS, tm, LANES), tile4),
                      pl.BlockSpec((None, B_WIDTH, tm), lambda i: (i, 0, 0))]
    else:
        out_shape += [
            jax.ShapeDtypeStruct((B_HEADS, T, KV_LORA), F32),
            jax.ShapeDtypeStruct((T, ROPE_T), F32),
            jax.ShapeDtypeStruct((T, KCAT), F32),
        ]
        out_specs += [pl.BlockSpec((B_HEADS, tm, KV_LORA), lambda i: (0, i, 0)),
                      pl.BlockSpec((tm, ROPE_T), row),
                      pl.BlockSpec((tm, KCAT), row)]
    return pl.pallas_call(
        _proj_in_prompt_kernel if prompt else _proj_in_sample_kernel,
        grid=(n_tiles,), in_specs=in_specs, out_specs=out_specs, out_shape=out_shape,
        compiler_params=_params(("parallel",)),
        name="proj_in_prompt" if prompt else "proj_in_sample",
    )(x, w["g1"], *weights, cos_tab, sin_tab)


def _cumsum_rows(g):
    n = g.shape[0]
    row = lax.broadcasted_iota(jnp.int32, g.shape, 0)
    b = g
    shift = 1
    while shift < n:
        b = b + jnp.where(row >= shift, pltpu.roll(b, shift, axis=0), 0.0)
        shift *= 2
    return b


def _hgrn_chunk(q, fr, v, lb, st, chunk, sub):
    n_sub = chunk // sub
    md = _operand_dtype(chunk)
    md_sub = _operand_dtype(sub)
    one_m_lb = 1.0 - lb
    f_gate = lb + one_m_lb * jax.nn.sigmoid(fr)
    g = jnp.log(jnp.maximum(f_gate, F_FLOOR))
    k = one_m_lb * jax.nn.sigmoid(-fr)
    b = _cumsum_rows(g)

    o = _dot_nt((q * jnp.exp(b)).astype(md), st.astype(md))
    b_end = b[chunk - 1:chunk, :]
    kd = (k * jnp.exp(b_end - b)).astype(md)
    st_new = st * jnp.exp(b_end) + _dot(v.T.astype(md), kd)

    if n_sub > 1:
        vs = v.astype(md_sub)
        parts = [jnp.zeros((sub, A_DV), F32)]
        for i in range(1, n_sub):
            lo = i * sub
            r_i = b[lo - 1:lo, :]
            q_i = (q[lo:lo + sub] * jnp.exp(b[lo:lo + sub] - r_i)).astype(md_sub)
            k_i = (k[0:lo] * jnp.exp(r_i - b[0:lo])).astype(md_sub)
            a_i = _dot_nt(q_i, k_i).astype(md_sub)
            parts.append(_dot(a_i, vs[0:lo]))
        o = o + jnp.concatenate(parts, axis=0)

    q3 = q.reshape(n_sub, sub, A_DK)
    k3 = k.reshape(n_sub, sub, A_DK)
    b3 = (b * LOG2E).reshape(n_sub, sub, A_DK)
    v3 = v.astype(md).astype(F32).reshape(n_sub, sub, A_DV)
    tpos = lax.broadcasted_iota(jnp.int32, (n_sub, sub, A_DK), 1)
    od = jnp.zeros((n_sub, sub, A_DV), F32)
    for s in range(sub):
        d = jnp.where(tpos >= s, b3 - b3[:, s:s + 1, :], MASK_VALUE)
        wgt = q3 * k3[:, s:s + 1, :] * jnp.exp2(d)
        od = od + jnp.sum(wgt, axis=-1, keepdims=True) * v3[:, s:s + 1, :]
    return o + od.reshape(chunk, A_DV), st_new


def _hgrn_kernel(*refs, chunk, sub, has_s0):
    if has_s0:
        (qa_ref, fr_ref, v_ref, og_ref, lb_ref, ang_ref, s0_ref, oa_ref, sout_ref, st_ref) = refs
    else:
        (qa_ref, fr_ref, v_ref, og_ref, lb_ref, ang_ref, oa_ref, sout_ref, st_ref) = refs
        s0_ref = None
    j = pl.program_id(1)
    nb, lblk = qa_ref.shape[0], qa_ref.shape[1]
    n_chunks = lblk // chunk

    @pl.when(j == 0)
    def _():
        if has_s0:
            for n in range(nb):
                for hh in range(A_HEADS):
                    st_ref[n, hh] = s0_ref[n, hh].T
        else:
            st_ref[...] = jnp.zeros_like(st_ref)

    def seq_body(n, carry):
        def chunk_body(c, carry2):
            rows = pl.ds(pl.multiple_of(c * chunk, chunk), chunk)
            for hh in range(A_HEADS):
                cols = slice(hh * LANES, (hh + 1) * LANES)
                o, st_new = _hgrn_chunk(qa_ref[n, rows, cols], fr_ref[n, rows, cols], v_ref[n, rows, cols],
                                        lb_ref[:, cols], st_ref[n, hh], chunk, sub)
                st_ref[n, hh] = st_new
                on = _rms(o, ang_ref[:, cols])
                oa_ref[n, rows, cols] = (on * og_ref[n, rows, cols]).astype(oa_ref.dtype)
            return carry2
        return lax.fori_loop(0, n_chunks, chunk_body, carry, unroll=min(4, n_chunks))
    lax.fori_loop(0, nb, seq_body, 0)

    @pl.when(j == pl.num_programs(1) - 1)
    def _():
        for n in range(nb):
            for hh in range(A_HEADS):
                sout_ref[n, hh] = st_ref[n, hh].T


def _hgrn(qa, fr, v, og, lb, ang, s0, layer, nb, lblk, chunk, sub):
    B, L, _ = qa.shape
    has_s0 = s0 is not None
    grid = (B // nb, L // lblk)
    act = pl.BlockSpec((nb, lblk, A_WIDTH), lambda i, j: (i, j, 0))
    st_spec = pl.BlockSpec((nb, A_HEADS, A_DK, A_DV), lambda i, j: (i, 0, 0, 0))
    in_specs = [act, act, act, act, _full_spec((1, A_WIDTH)), _full_spec((1, A_WIDTH))]
    args = [qa, fr, v, og, lb, ang]
    if has_s0:
        in_specs.append(pl.BlockSpec((None, nb, A_HEADS, A_DK, A_DV), lambda i, j: (layer, i, 0, 0, 0)))
        args.append(s0)
    return pl.pallas_call(
        functools.partial(_hgrn_kernel, chunk=chunk, sub=sub, has_s0=has_s0),
        grid=grid, in_specs=in_specs,
        out_specs=[act, st_spec],
        out_shape=[jax.ShapeDtypeStruct((B, L, A_WIDTH), BF16),
                   jax.ShapeDtypeStruct((B, A_HEADS, A_DK, A_DV), F32)],
        scratch_shapes=[pltpu.VMEM((nb, A_HEADS, A_DV, A_DK), F32)],
        compiler_params=_params(("parallel", "arbitrary")),
        name="hgrn_sample" if has_s0 else "hgrn_prompt",
    )(*args)


def _attn_prompt_kernel(qt_ref, kh_ref, vt_ref, out_ref, m_s, l_s, acc_s, *, tm, tiles, heads_per_group):
    qi = pl.program_id(1)
    tq = tiles * tm
    gw = heads_per_group * tq
    m_s[...] = jnp.full_like(m_s, -jnp.inf)
    l_s[...] = jnp.zeros_like(l_s)
    acc_s[...] = jnp.zeros_like(acc_s)

    def tile(kt, diagonal):
        vt = jnp.concatenate([vt_ref[kt * tiles + j] for j in range(tiles)], axis=1)
        for grp in range(B_HEADS // heads_per_group):
            heads = range(grp * heads_per_group, (grp + 1) * heads_per_group)
            gcols = slice(grp * gw, (grp + 1) * gw)
            parts = []
            for hh in heads:
                keys = jnp.concatenate([kh_ref[kt * tiles + j, hh] for j in range(tiles)], axis=0)
                qry = jnp.concatenate([qt_ref[j, hh] for j in range(tiles)], axis=1)
                parts.append(_dot(keys, qry))
            st = jnp.concatenate(parts, axis=1)
            if diagonal:
                kpos = lax.broadcasted_iota(jnp.int32, st.shape, 0)
                qpos = lax.broadcasted_iota(jnp.int32, st.shape, 1) % tq
                st = jnp.where(kpos <= qpos, st, MASK_VALUE)
            m_prev = m_s[:, gcols]
            m_new = jnp.maximum(m_prev, jnp.max(st, axis=0, keepdims=True))
            alpha = jnp.exp(m_prev - m_new)
            p = jnp.exp(st - m_new)
            l_s[:, gcols] = alpha * l_s[:, gcols] + jnp.sum(p, axis=0, keepdims=True)
            m_s[:, gcols] = m_new
            pb = p.astype(BF16)
            for i, hh in enumerate(heads):
                cols = slice(i * tq, (i + 1) * tq)
                rows = slice(hh * B_DV, (hh + 1) * B_DV)
                acc_s[rows, :] = alpha[:, cols] * acc_s[rows, :] + _dot(vt[rows, :], pb[:, cols])

    def body(kt, carry):
        tile(kt, False)
        return carry
    lax.fori_loop(0, qi, body, 0)
    tile(qi, True)

    o_t = jnp.concatenate([acc_s[hh * B_DV:(hh + 1) * B_DV, :] / l_s[:, hh * tq:(hh + 1) * tq]
                           for hh in range(B_HEADS)], axis=0)
    out_ref[...] = o_t.T.astype(out_ref.dtype)


def _attn_prompt(qt, kh, vt, B, L, tm, tiles, heads_per_group):
    tq = tiles * tm
    nq = L // tq
    n_tm = L // tm
    return pl.pallas_call(
        functools.partial(_attn_prompt_kernel, tm=tm, tiles=tiles, heads_per_group=heads_per_group),
        grid=(B, nq),
        in_specs=[
            pl.BlockSpec((tiles, B_HEADS, LANES, tm), lambda b, i: (b * nq + i, 0, 0, 0)),
            pl.BlockSpec((n_tm, B_HEADS, tm, LANES), lambda b, i: (b, 0, 0, 0)),
            pl.BlockSpec((n_tm, B_WIDTH, tm), lambda b, i: (b, 0, 0)),
        ],
        out_specs=pl.BlockSpec((tq, B_WIDTH), lambda b, i: (b * nq + i, 0)),
        out_shape=jax.ShapeDtypeStruct((B * L, B_WIDTH), BF16),
        scratch_shapes=[pltpu.VMEM((1, B_HEADS * tq), F32), pltpu.VMEM((1, B_HEADS * tq), F32),
                        pltpu.VMEM((B_WIDTH, tq), F32)],
        compiler_params=_params(("parallel", "arbitrary")),
        name="attn_prompt",
    )(qt, kh, vt)


def _lane_tile(x, width):
    if width % LANES == 0:
        return jnp.concatenate([x] * (width // LANES), axis=1) if width > LANES else x
    return x[:, 0:width]


def _attn_sample_kernel(pt_ref, qlat_ref, qpe_ref, knew_ref, wuv_ref, ckv_hbm, krt_hbm, out_ref,
                        q_s, qp_s, k_s, kt_s, kpt_s, m_s, l_s, acc_s, kbuf, krbuf, sem,
                        *, layer, n_pages, n_seq_blk, dec):
    blk = pl.program_id(0)
    g = pl.program_id(1)
    n_groups = pl.num_programs(1)
    step = blk * n_groups + g
    n_steps = pl.num_programs(0) * n_groups
    slot = lax.rem(step, PAGE_SLOTS)
    rows = B_HEADS * dec

    def page_copies(slot_, page_of):
        out = []
        for sq in range(n_seq_blk):
            for i in range(n_pages):
                page = page_of(sq, i)
                keys = pl.ds(i * PAGE_SIZE, PAGE_SIZE)
                out.append(pltpu.make_async_copy(ckv_hbm.at[layer, page], kbuf.at[slot_, sq, keys, :], sem.at[slot_]))
                out.append(pltpu.make_async_copy(krt_hbm.at[layer, page], krbuf.at[slot_, sq, :, keys], sem.at[slot_]))
        return out

    def start_step(step_, slot_):
        blk_ = step_ // n_groups
        g_ = step_ - blk_ * n_groups
        copies = page_copies(slot_, lambda sq, i: pt_ref[blk_ * n_seq_blk + sq, g_ * n_pages + i])
        for k, c in enumerate(copies):
            c.start(priority=(k // 2) % 2)

    ahead = PAGE_SLOTS - 1

    @pl.when(step == 0)
    def _():
        for d in range(ahead):
            @pl.when(d < n_steps)
            def _():
                start_step(d, d)

    @pl.when(step + ahead < n_steps)
    def _():
        start_step(step + ahead, lax.rem(step + ahead, PAGE_SLOTS))

    for c in page_copies(slot, lambda sq, i: 0):
        c.wait()

    @pl.when(g == 0)
    def _():
        for sq in range(n_seq_blk):
            seq = blk * n_seq_blk + sq
            q_s[sq] = qlat_ref[:, seq].reshape(rows, KV_LORA).astype(BF16)
            qpe = qpe_ref[seq]
            qp_s[sq] = jnp.concatenate(
                [qpe[:, hh * B_ROPE:(hh + 1) * B_ROPE] for hh in range(B_HEADS)], axis=0).astype(BF16)
        m_s[...] = jnp.full_like(m_s, -jnp.inf)
        l_s[...] = jnp.zeros_like(l_s)
        acc_s[...] = jnp.zeros_like(acc_s)

    def update(sq, s, vals):
        n = s.shape[1]
        m_prev = m_s[sq]
        m_new = jnp.maximum(m_prev, jnp.max(s, axis=-1, keepdims=True))
        alpha = jnp.exp(m_prev - m_new)
        p = jnp.exp(s - _lane_tile(m_new, n))
        l_s[sq] = alpha * l_s[sq] + jnp.sum(p, axis=-1, keepdims=True)
        acc_s[sq] = _lane_tile(alpha, KV_LORA) * acc_s[sq] + _dot(p.astype(vals.dtype), vals)
        m_s[sq] = m_new

    for sq in range(n_seq_blk):
        for i in range(n_pages):
            page = kbuf[slot, sq, i * PAGE_SIZE:(i + 1) * PAGE_SIZE, :]
            k_s[sq, i * PAGE_SIZE:(i + 1) * PAGE_SIZE, :] = page.astype(BF16)
            kt_s[sq, :, i * PAGE_SIZE:(i + 1) * PAGE_SIZE] = page.T.astype(BF16)
        kpt_s[sq] = krbuf[slot, sq].astype(BF16)
        kc = k_s[sq]
        update(sq, _dot(q_s[sq], kt_s[sq]) + _dot(qp_s[sq], kpt_s[sq]), kc)

    @pl.when(g == pl.num_programs(1) - 1)
    def _():
        qpos = lax.broadcasted_iota(jnp.int32, (rows, dec), 0) % dec
        kpos = lax.broadcasted_iota(jnp.int32, (rows, dec), 1)
        for sq in range(n_seq_blk):
            seq = blk * n_seq_blk + sq
            knew = knew_ref[seq]
            kc_new = knew[:, 0:KV_LORA]
            kp_new = knew[:, KV_LORA:KV_LORA + B_ROPE]
            s = _dot_nt(q_s[sq].astype(F32), kc_new) + _dot_nt(qp_s[sq].astype(F32), kp_new)
            update(sq, jnp.where(kpos <= qpos, s, MASK_VALUE), kc_new)
            o = (acc_s[sq] / _lane_tile(l_s[sq], KV_LORA)).astype(_operand_dtype(dec))
            for p in range(B_HEADS // 2):
                pair = jnp.concatenate([o[(2 * p) * dec:(2 * p + 1) * dec],
                                        o[(2 * p + 1) * dec:(2 * p + 2) * dec]], axis=-1)
                out_ref[seq, :, p * LANES:(p + 1) * LANES] = _dot(pair, wuv_ref[p].astype(pair.dtype))


def _attn_sample(page_table, qlat, qpe, kcat, cache_ckv, cache_krope_t, wuv, layer, n_pages, n_seq_blk):
    n_seq, n_logical = page_table.shape
    dec = qlat.shape[1] // n_seq
    rows = B_HEADS * dec
    groups = n_logical // n_pages

    whole3 = lambda n, g, pt: (0, 0, 0)
    in_specs = [
        pl.BlockSpec((B_HEADS, n_seq, dec, KV_LORA), lambda n, g, pt: (0, 0, 0, 0)),
        pl.BlockSpec((n_seq, dec, ROPE_T), whole3),
        pl.BlockSpec((n_seq, dec, KCAT), whole3),
        pl.BlockSpec(wuv.shape, whole3),
        pl.BlockSpec(memory_space=pl.ANY),
        pl.BlockSpec(memory_space=pl.ANY),
    ]
    keys = n_pages * PAGE_SIZE
    grid_spec = pltpu.PrefetchScalarGridSpec(
        num_scalar_prefetch=1, grid=(n_seq // n_seq_blk, groups), in_specs=in_specs,
        out_specs=pl.BlockSpec((n_seq, dec, B_WIDTH), whole3),
        scratch_shapes=[pltpu.VMEM((n_seq_blk, rows, KV_LORA), BF16), pltpu.VMEM((n_seq_blk, rows, B_ROPE), BF16),
                        pltpu.VMEM((n_seq_blk, keys, KV_LORA), BF16),
                        pltpu.VMEM((n_seq_blk, KV_LORA, keys), BF16),
                        pltpu.VMEM((n_seq_blk, B_ROPE, keys), BF16),
                        pltpu.VMEM((n_seq_blk, rows, LANES), F32), pltpu.VMEM((n_seq_blk, rows, LANES), F32),
                        pltpu.VMEM((n_seq_blk, rows, KV_LORA), F32),
                        pltpu.VMEM((PAGE_SLOTS, n_seq_blk, keys, KV_LORA), F32),
                        pltpu.VMEM((PAGE_SLOTS, n_seq_blk, B_ROPE, keys), F32),
                        pltpu.SemaphoreType.DMA((PAGE_SLOTS,))])
    out = pl.pallas_call(
        functools.partial(_attn_sample_kernel, layer=layer, n_pages=n_pages, n_seq_blk=n_seq_blk, dec=dec),
        grid_spec=grid_spec,
        out_shape=jax.ShapeDtypeStruct((n_seq, dec, B_WIDTH), F32),
        compiler_params=_params(("arbitrary", "arbitrary")),
        name="attn_sample",
    )(page_table, qlat.reshape(B_HEADS, n_seq, dec, KV_LORA), qpe.reshape(n_seq, dec, ROPE_T),
      kcat.reshape(n_seq, dec, KCAT), wuv, cache_ckv, cache_krope_t)
    return out.reshape(n_seq * dec, B_WIDTH)


def _post_kernel(x_ref, oa_ref, ob_ref, ga_ref, gb_ref, wbra_ref, wbrb_ref, wout_ref, g2_ref,
                 wup_ref, wdn_ref, gf_ref, out_ref, *, final):
    ma = _dot(oa_ref[...].astype(BF16), wbra_ref[...])
    mb = _dot(ob_ref[...].astype(BF16), wbrb_ref[...])
    merged = ga_ref[...].astype(F32) * ma + gb_ref[...].astype(F32) * mb
    x1 = x_ref[...] + _dot(merged.astype(BF16), wout_ref[...])
    h2 = _rms(x1, g2_ref[...]).astype(BF16)
    acc = x1
    ff_chunk = D_MODEL
    for c in range(D_FF // ff_chunk):
        u = jnp.maximum(_dot(h2, wup_ref[:, c * ff_chunk:(c + 1) * ff_chunk]), 0.0)
        acc = acc + _dot((u * u).astype(BF16), wdn_ref[c * ff_chunk:(c + 1) * ff_chunk, :])
    if final:
        acc = _rms(acc, gf_ref[...])
    out_ref[...] = acc


def _layer_spec(stacked, layer):
    return pl.BlockSpec((None,) + stacked.shape[1:], lambda i: (layer, 0, 0), pipeline_mode=pl.Buffered(1))


def _post(x, oa, ob, ga, gb, w, ws, layer, gf, tm, final):
    T = x.shape[0]
    row = lambda i: (i, 0)
    stacks = [ws["wbra"], ws["wbrb"], ws["wout"]]
    in_specs = [
        pl.BlockSpec((tm, D_MODEL), row), pl.BlockSpec((tm, A_WIDTH), row), pl.BlockSpec((tm, B_WIDTH), row),
        pl.BlockSpec((tm, D_MODEL), row), pl.BlockSpec((tm, D_MODEL), row),
        _layer_spec(ws["wbra"], layer), _layer_spec(ws["wbrb"], layer), _layer_spec(ws["wout"], layer),
        _full_spec((1, D_MODEL)), _layer_spec(ws["wup"], layer), _layer_spec(ws["wdn"], layer),
        _full_spec((1, D_MODEL)),
    ]
    return pl.pallas_call(
        functools.partial(_post_kernel, final=final),
        grid=(T // tm,), in_specs=in_specs,
        out_specs=pl.BlockSpec((tm, D_MODEL), row),
        out_shape=jax.ShapeDtypeStruct((T, D_MODEL), F32),
        compiler_params=_params(("parallel",)),
        name="post_final" if final else "post",
    )(x, oa, ob, ga, gb, *stacks, w["g2"], ws["wup"], ws["wdn"], gf)


def _rot_half(w):
    half = w.shape[-1] // 2
    return jnp.concatenate([-w[..., half:], w[..., :half]], axis=-1)


def _layer_weights(l, lbs, norm1_g, w_in, a_norm_g, q_norm_g, w_uq, kv_norm_g, w_uk, w_uv,
                   w_br_a, w_br_b, w_out, norm2_g, w_up, w_down):
    wi = w_in[l]
    c_b = 4 * A_WIDTH
    c_rope = c_b + Q_LORA + KV_LORA
    c_g = c_rope + B_ROPE
    rope = wi[:, c_rope:c_g]
    wb = jnp.concatenate([wi[:, c_b:c_rope], jnp.tile(rope, (1, B_HEADS)),
                          jnp.tile(_rot_half(rope), (1, B_HEADS))], axis=-1)
    uq = w_uq[l].reshape(Q_LORA, B_HEADS, B_NOPE + B_ROPE)
    nope = uq[:, :, :B_NOPE]
    wqn = jnp.pad(nope, ((0, 0), (0, 0), (0, LANES - B_NOPE))).reshape(Q_LORA, B_HEADS * LANES)
    pe = uq[:, :, B_NOPE:]
    tail = LANES - B_ROPE - B_NOPE
    wqf = jnp.pad(jnp.concatenate([pe, nope], axis=-1), ((0, 0), (0, 0), (0, tail))).reshape(Q_LORA, B_HEADS * LANES)
    wqr = jnp.pad(_rot_half(pe), ((0, 0), (0, 0), (0, LANES - B_ROPE))).reshape(Q_LORA, B_HEADS * LANES)
    lane_pad = ((0, 0), (0, LANES - B_ROPE))
    wbp = jnp.concatenate([wi[:, c_b:c_rope], jnp.pad(rope, lane_pad), jnp.pad(_rot_half(rope), lane_pad)], axis=-1)
    wukf = jnp.pad(w_uk[l], ((0, 0), (0, 0), (B_ROPE, tail)))
    wukf = jnp.swapaxes(wukf, 0, 1).reshape(KV_LORA, B_HEADS * LANES)
    wuvf = jnp.swapaxes(w_uv[l], 1, 2).reshape(B_WIDTH, KV_LORA)
    wqp = jnp.concatenate([pe.reshape(Q_LORA, ROPE_T), _rot_half(pe).reshape(Q_LORA, ROPE_T)], axis=-1)
    wuk = jnp.pad(jnp.swapaxes(w_uk[l], 1, 2), ((0, 0), (0, LANES - B_NOPE), (0, 0)))
    uv = w_uv[l]
    zeros = jnp.zeros_like(uv[0])
    wuv = jnp.stack([jnp.concatenate([jnp.concatenate([uv[2 * p], zeros], axis=-1),
                                      jnp.concatenate([zeros, uv[2 * p + 1]], axis=-1)], axis=0)
                     for p in range(B_HEADS // 2)])
    bf = lambda a: a.astype(BF16)
    return dict(
        g1=norm1_g[l].reshape(1, D_MODEL), wa=bf(wi[:, :c_b]), wb=bf(wb), wg=bf(wi[:, c_g:]),
        qn=q_norm_g[l].reshape(1, Q_LORA), kvn=kv_norm_g[l].reshape(1, KV_LORA),
        wqn=bf(wqn), wqp=bf(wqp), wuk=bf(wuk), wuv=bf(wuv),
        wbp=bf(wbp), wqf=bf(wqf), wqr=bf(wqr), wukf=bf(wukf), wuvf=bf(wuvf),
        lb=lbs[l].reshape(1, A_WIDTH), ang=a_norm_g[l].reshape(1, A_WIDTH),
        g2=norm2_g[l].reshape(1, D_MODEL),
    )


def _rope_angles(pos):
    half = B_ROPE // 2
    inv = ROPE_THETA ** (-jnp.arange(half, dtype=F32) / half)
    ang = pos.astype(F32)[:, None] * inv[None, :]
    return jnp.tile(jnp.cos(ang), (1, 2)), jnp.tile(jnp.sin(ang), (1, 2))


def _rope_tables_tiled(pos):
    cos, sin = _rope_angles(pos)
    return jnp.tile(cos, (1, B_HEADS)), jnp.tile(sin, (1, B_HEADS))


def _rope_tables_block(pos):
    cos, sin = _rope_angles(pos)
    pad = ((0, 0), (0, LANES - B_ROPE))
    return jnp.pad(cos, pad, constant_values=1.0), jnp.pad(sin, pad)


def kernel(x_prompt, x_sample, cache_ckv, cache_krope, state_hgrn, page_table, norm1_g, w_in, lower_bounds,
           a_norm_g, q_norm_g, w_uq, kv_norm_g, w_uk, w_uv, w_br_a, w_br_b, w_out, norm2_g, w_up, w_down,
           final_norm_g):
    B, L, _ = x_prompt.shape
    n_seq, dec, _ = x_sample.shape
    n_logical = page_table.shape[1]
    past_len = n_logical * PAGE_SIZE
    tq = 256
    tm_s = 256
    pages_per_step = min(32, n_logical)
    seqs_per_step = 2

    lbs = _lower_bounds(lower_bounds)
    cos_p, sin_p = _rope_tables_block(jnp.arange(L, dtype=jnp.int32))
    pos_s = past_len + (jnp.arange(tm_s, dtype=jnp.int32) % dec)
    cos_s, sin_s = _rope_tables_tiled(pos_s)
    gf = final_norm_g.reshape(1, D_MODEL)
    cache_krope_t = jnp.swapaxes(cache_krope, 2, 3)

    xp = x_prompt.reshape(B * L, D_MODEL)
    xs = x_sample.reshape(n_seq * dec, D_MODEL)
    outs = [[] for _ in range(6)]
    w_in_b = w_in.astype(BF16)
    ws = dict(wbra=w_br_a.astype(BF16), wbrb=w_br_b.astype(BF16), wout=w_out.astype(BF16),
              wup=w_up.astype(BF16), wdn=w_down.astype(BF16))
    for l in range(DEPTH):
        w = _layer_weights(l, lbs, norm1_g, w_in_b, a_norm_g, q_norm_g, w_uq, kv_norm_g, w_uk, w_uv,
                           w_br_a, w_br_b, w_out, norm2_g, w_up, w_down)
        final = l == DEPTH - 1

        qa, fr, v, og, ga, gb, ckv, kpe, qt, kh, vt = _proj_in(xp, w, cos_p, sin_p, tq, True)
        to_seq = lambda a: a.reshape(B, L, A_WIDTH)
        oa, s_p = _hgrn(to_seq(qa), to_seq(fr), to_seq(v), to_seq(og), w["lb"], w["ang"], None, l,
                        nb=1, lblk=256, chunk=64, sub=16)
        ob = _attn_prompt(qt, kh, vt, B, L, tq, tiles=2, heads_per_group=8)
        xp = _post(xp, oa.reshape(B * L, A_WIDTH), ob, ga, gb, w, ws, l, gf, tq, final)
        outs[0].append(ckv.reshape(B, L, KV_LORA))
        outs[1].append(kpe.reshape(B, L, B_ROPE))
        outs[2].append(s_p)

        qa, fr, v, og, ga, gb, ckv, kpe, qlat, qpe, kcat = _proj_in(xs, w, cos_s, sin_s, tm_s, False)
        to_seq = lambda a: a.reshape(n_seq, dec, A_WIDTH)
        oa, s_s = _hgrn(to_seq(qa), to_seq(fr), to_seq(v), to_seq(og), w["lb"], w["ang"], state_hgrn, l,
                        nb=8, lblk=dec, chunk=dec, sub=dec)
        ob = _attn_sample(page_table, qlat, qpe, kcat, cache_ckv, cache_krope_t, w["wuv"], l, pages_per_step,
                          seqs_per_step)
        xs = _post(xs, oa.reshape(n_seq * dec, A_WIDTH), ob, ga, gb, w, ws, l, gf, tm_s, final)
        outs[3].append(ckv.reshape(n_seq, dec, KV_LORA))
        outs[4].append(kpe.reshape(n_seq, dec, B_ROPE))
        outs[5].append(s_s)

    return (xp.reshape(B, L, D_MODEL), xs.reshape(n_seq, dec, D_MODEL),
            jnp.stack(outs[0]), jnp.stack(outs[1]), jnp.stack(outs[2]),
            jnp.stack(outs[3]), jnp.stack(outs[4]), jnp.stack(outs[5]))
```
